```python
import math
import jax
import jax.numpy as jnp
from jax import lax
import numpy as np

D_MODEL = 2048
BATCH = 1
SEQ = 16384
DEPTH = 2

GRID_W = 64
CTX_LEN = 256
N_EVEN = (DEPTH + 1) // 2
N_ODD = DEPTH // 2

HY_WIDTH = D_MODEL // 2
HY_ORDER = 2
HY_EMB = 33
HY_BANDS = (HY_EMB - 1) // 2
HY_FILTER_HIDDEN = 64
HY_DECAY_TARGET = 1e-2
HY_FAST_PCT = 0.3
HY_SLOW_PCT = 1.5
HY_SHORT = 3
NA_HEADS = 8
NA_HEAD_DIM = (D_MODEL // 2) // NA_HEADS
NA_WIN_ROWS = 8
NA_WIN_COLS = 16
ML_HEADS = 8
ML_V_DIM = D_MODEL // ML_HEADS
ML_QK_DIM = ML_V_DIM // 2
ML_CHUNK = 64
ML_SHORT = 3
ML_M0 = -1e30
ROPE_BASE = 10000.0
PEER_HEADS = 8
PEER_NKEYS = 128
PEER_EXPERTS = PEER_NKEYS * PEER_NKEYS
PEER_TOPK = 16
PEER_QDIM = 256
PEER_BLOCK = 128

LN_EPS = 1e-6
DEEPNORM_ALPHA = (2 * DEPTH) ** 0.25
DEEPNORM_BETA = (8 * DEPTH) ** -0.25

kernel_name = 'hybrid_hyena_natten_mlstm_peer_dit'


def layer_norm(x, w, b):
    xf = x.astype(jnp.float32)
    mu = jnp.mean(xf, axis=-1, keepdims=True)
    var = jnp.mean(jnp.square(xf - mu), axis=-1, keepdims=True)
    return ((xf - mu) * lax.rsqrt(var + LN_EPS) * w + b).astype(x.dtype)


def short_conv(z, w, b):
    n_taps, seq_len = w.shape[0], z.shape[1]
    pad = n_taps // 2
    zp = jnp.pad(z, ((0, 0), (pad, pad), (0, 0)))
    out = b
    for j in range(n_taps):
        out = out + zp[:, j:j + seq_len] * w[j]
    return out


def hyena_filters(seq_len, w1, b1, w2, b2, w3, b3, w4):
    t = jnp.linspace(0.0, 1.0, seq_len, dtype=jnp.float32)[:, None]
    omega = 2.0 * math.pi * jnp.arange(seq_len, dtype=jnp.float32) / seq_len
    freqs = jnp.linspace(1e-4, HY_BANDS - 1, HY_BANDS, dtype=jnp.float32)
    ang = omega[:, None] * freqs[None, :]
    feats = jnp.concatenate([t, jnp.cos(ang), -jnp.sin(ang)], axis=-1).astype(w1.dtype)
    h = jnp.sin(feats @ w1 + b1)
    h = jnp.sin(h @ w2 + b2)
    h = jnp.sin(h @ w3 + b3)
    h = (h @ w4).astype(jnp.float32).reshape(seq_len, 2, HY_ORDER, HY_WIDTH)
    max_decay = math.log(HY_DECAY_TARGET) / HY_FAST_PCT
    min_decay = math.log(HY_DECAY_TARGET) / HY_SLOW_PCT
    deltas = jnp.abs(jnp.linspace(min_decay, max_decay, HY_WIDTH, dtype=jnp.float32))
    window = jnp.exp(-t * deltas[None, :])
    return h * window[:, None, None, :]


def bidir_long_conv(z, h_fwd, h_bwd, skip):
    seq_len, ch = z.shape[1], z.shape[2]
    taps = jnp.concatenate([h_fwd, jnp.zeros((1, ch), jnp.float32), h_bwd[:0:-1]], axis=0)
    zf = z.astype(jnp.float32)
    spec = jnp.fft.rfft(zf, n=2 * seq_len, axis=1) * jnp.fft.rfft(taps, axis=0)[None]
    y = jnp.fft.irfft(spec, n=2 * seq_len, axis=1)[:, :seq_len]
    return (y + zf * skip.astype(jnp.float32)).astype(z.dtype)


def hyena_mix(proj, conv_w, conv_b, filt, skip):
    zc = short_conv(proj, conv_w, conv_b)
    z = zc[..., :HY_WIDTH]
    for o in range(HY_ORDER):
        gate = zc[..., (o + 1) * HY_WIDTH:(o + 2) * HY_WIDTH]
        z = gate * bidir_long_conv(z, filt[:, 0, o], filt[:, 1, o], skip[o])
    return z


def rope_2d(x):
    seq_len, d = x.shape[1], x.shape[-1]
    quarter = d // 4
    pos = jnp.arange(seq_len)
    inv = ROPE_BASE ** (-jnp.arange(quarter, dtype=jnp.float32) / quarter)

    def rot(y, p):
        ang = p.astype(jnp.float32)[:, None] * inv[None]
        cos, sin = jnp.cos(ang)[None, :, None], jnp.sin(ang)[None, :, None]
        y1, y2 = y[..., :quarter], y[..., quarter:]
        return jnp.concatenate([y1 * cos - y2 * sin, y1 * sin + y2 * cos], axis=-1)

    xf = x.astype(jnp.float32)
    out = jnp.concatenate([rot(xf[..., :2 * quarter], pos // GRID_W), rot(xf[..., 2 * quarter:], pos % GRID_W)], axis=-1)
    return out.astype(x.dtype)


def neighbourhood_attention(q, k, v, k_ctx, v_ctx, rpb):
    bsz, seq_len, heads, hd = q.shape
    rows = seq_len // GRID_W
    kr, kc = min(NA_WIN_ROWS, rows), NA_WIN_COLS
    scale = hd ** -0.5
    qg = q.reshape(bsz, rows, GRID_W, heads, hd)
    kg = k.reshape(bsz, rows, GRID_W, heads, hd)
    vg = v.reshape(bsz, rows, GRID_W, heads, hd)
    cols = np.arange(GRID_W)
    col_idx = np.clip(cols - kc // 2, 0, GRID_W - kc)[:, None] + np.arange(kc)[None]
    col_off = col_idx - cols[:, None] + NA_WIN_COLS - 1
    rpb_cols = rpb[:, :, col_off]

    def row_block(r):
        r0 = jnp.clip(r - kr // 2, 0, rows - kr)
        q_r = lax.dynamic_index_in_dim(qg, r, axis=1, keepdims=False)
        k_nb = lax.dynamic_slice_in_dim(kg, r0, kr, axis=1)[:, :, col_idx]
        v_nb = lax.dynamic_slice_in_dim(vg, r0, kr, axis=1)[:, :, col_idx]
        row_off = r0 + jnp.arange(kr) - r + NA_WIN_ROWS - 1
        bias = jnp.take(rpb_cols, row_off, axis=1).transpose(0, 2, 1, 3)
        s_lat = jnp.einsum('bqhd,brqkhd->bhqrk', q_r, k_nb) * scale + bias[None]
        s_ctx = jnp.einsum('bqhd,bchd->bhqc', q_r, k_ctx) * scale
        s = jnp.concatenate([s_lat.reshape(bsz, heads, GRID_W, kr * kc), s_ctx], axis=-1)
        p = jax.nn.softmax(s.astype(jnp.float32), axis=-1).astype(v.dtype)
        p_lat = p[..., :kr * kc].reshape(bsz, heads, GRID_W, kr, kc)
        return (jnp.einsum('bhqrk,brqkhd->bqhd', p_lat, v_nb)
                + jnp.einsum('bhqc,bchd->bqhd', p[..., kr * kc:], v_ctx))

    out = lax.map(row_block, jnp.arange(rows))
    return jnp.moveaxis(out, 0, 1).reshape(bsz, seq_len, heads * hd)


def context_attention(q, k, v):
    bsz, n, heads, hd = q.shape
    s = jnp.einsum('bqhd,bkhd->bhqk', q, k) * hd ** -0.5
    p = jax.nn.softmax(s.astype(jnp.float32), axis=-1).astype(v.dtype)
    return jnp.einsum('bhqk,bkhd->bqhd', p, v).reshape(bsz, n, heads * hd)


def hyena_na_mixer(u_lat, u_ctx, w_in, conv_w, conv_b, fw1, fb1, fw2, fb2, fw3, fb3, fw4, skip, rpb, w_out, with_ctx):
    bsz, seq_len, _ = u_lat.shape
    n_ctx = u_ctx.shape[1]
    split = 3 * HY_WIDTH
    na_w = NA_HEADS * NA_HEAD_DIM
    p_lat = u_lat @ w_in
    hy_lat = hyena_mix(p_lat[..., :split], conv_w, conv_b,
                       hyena_filters(seq_len, fw1, fb1, fw2, fb2, fw3, fb3, fw4), skip)
    qkv = p_lat[..., split:].reshape(bsz, seq_len, 3, NA_HEADS, NA_HEAD_DIM)
    if with_ctx:
        p_ctx = u_ctx @ w_in
        qkv_c = p_ctx[..., split:].reshape(bsz, n_ctx, 3, NA_HEADS, NA_HEAD_DIM)
        k_c, v_c = qkv_c[:, :, 1], qkv_c[:, :, 2]
    else:
        kv_c = (u_ctx @ w_in[:, split + na_w:]).reshape(bsz, n_ctx, 2, NA_HEADS, NA_HEAD_DIM)
        k_c, v_c = kv_c[:, :, 0], kv_c[:, :, 1]
    na_lat = neighbourhood_attention(qkv[:, :, 0], qkv[:, :, 1], qkv[:, :, 2], k_c, v_c, rpb)
    y_lat = jnp.concatenate([hy_lat, na_lat], axis=-1) @ w_out
    if not with_ctx:
        return y_lat, None
    hy_ctx = hyena_mix(p_ctx[..., :split], conv_w, conv_b,
                       hyena_filters(n_ctx, fw1, fb1, fw2, fb2, fw3, fb3, fw4), skip)
    na_ctx = context_attention(qkv_c[:, :, 0], k_c, v_c)
    y_ctx = jnp.concatenate([hy_ctx, na_ctx], axis=-1) @ w_out
    return y_lat, y_ctx


def mlstm_chunkwise(k, v, log_i, log_f, state, q=None):
    bsz, seq_len, heads, _ = k.shape
    n_chunks = seq_len // ML_CHUNK

    def chunks(a):
        a = a.astype(jnp.float32)
        return jnp.moveaxis(a.reshape(bsz, n_chunks, ML_CHUNK, *a.shape[2:]), 1, 0)

    xs = (chunks(k), chunks(v), chunks(log_i), chunks(log_f))
    if q is not None:
        xs = xs + (chunks(q),)
    lower = jnp.tril(jnp.ones((ML_CHUNK, ML_CHUNK), dtype=bool))

    def step(carry, xc):
        c_mat, n_vec, m = carry
        kc, vc, ic, fc = xc[:4]
        b = jnp.cumsum(fc, axis=1)
        b_end = b[:, -1]
        w_log = b_end[:, None] - b + ic
        m_new = jnp.maximum(b_end + m, jnp.max(w_log, axis=1))
        carry_decay = jnp.exp(b_end + m - m_new)
        w = jnp.exp(w_log - m_new[:, None])
        c_new = carry_decay[..., None, None] * c_mat + jnp.einsum('bshv,bshd->bhvd', vc * w[..., None], kc)
        n_new = carry_decay[..., None] * n_vec + jnp.einsum('bsh,bshd->bhd', w, kc)
        if q is None:
            return (c_new, n_new, m_new), None
        qc = xc[4]
        log_d = jnp.where(lower[None, :, :, None], b[:, :, None] - b[:, None] + ic[:, None], -jnp.inf)
        m_in = b + m[:, None]
        m_t = jnp.maximum(m_in, jnp.max(log_d, axis=2))
        scores = jnp.einsum('bthd,bshd->btsh', qc, kc) * jnp.exp(log_d - m_t[:, :, None])
        g_in = jnp.exp(m_in - m_t)
        num = (jnp.einsum('btsh,bshv->bthv', scores, vc)
               + jnp.einsum('bthd,bhvd->bthv', qc, c_mat) * g_in[..., None])
        den = jnp.sum(scores, axis=2) + jnp.einsum('bthd,bhd->bth', qc, n_vec) * g_in
        h = num / jnp.maximum(jnp.abs(den), jnp.exp(-m_t))[..., None]
        return (c_new, n_new, m_new), h

    state, hs = lax.scan(step, state, xs)
    if q is None:
        return state, None
    return state, jnp.moveaxis(hs, 0, 1).reshape(bsz, seq_len, heads, v.shape[-1])


def _flip(a):
    return None if a is None else jnp.flip(a, axis=1)


def mlstm_mixer(u_lat, u_ctx, w_in, b_gates, conv_w, conv_b, norm_w, w_out, with_ctx):
    bsz = u_lat.shape[0]
    heads, dk, dv = ML_HEADS, ML_QK_DIM, ML_V_DIM
    qk_w, v_w = heads * dk, heads * dv
    g0 = 2 * qk_w + 2 * v_w
    k_scale = dk ** -0.5

    def gate_terms(g):
        g = (g + b_gates).astype(jnp.float32).reshape(g.shape[0], g.shape[1], 4, heads)
        return (g[..., 0, :], jax.nn.log_sigmoid(g[..., 1, :]), g[..., 2, :], jax.nn.log_sigmoid(g[..., 3, :]))

    def split_heads(a, d):
        return a.reshape(a.shape[0], a.shape[1], heads, d)

    p = u_lat @ w_in
    qk = jax.nn.silu(short_conv(p[..., :2 * qk_w], conv_w, conv_b))
    q_l = rope_2d(split_heads(qk[..., :qk_w], dk))
    k_l = rope_2d(split_heads(qk[..., qk_w:], dk)) * k_scale
    v_l = split_heads(p[..., 2 * qk_w:2 * qk_w + v_w], dv)
    o_l = p[..., 2 * qk_w + v_w:g0]
    g_l = gate_terms(p[..., g0:])
    if with_ctx:
        pc = u_ctx @ w_in
        qkc = jax.nn.silu(short_conv(pc[..., :2 * qk_w], conv_w, conv_b))
        q_c = split_heads(qkc[..., :qk_w], dk)
        k_c = split_heads(qkc[..., qk_w:], dk) * k_scale
        v_c = split_heads(pc[..., 2 * qk_w:2 * qk_w + v_w], dv)
        o_c = pc[..., 2 * qk_w + v_w:g0]
        g_c = gate_terms(pc[..., g0:])
    else:
        pc = u_ctx @ w_in[:, qk_w:2 * qk_w + v_w]
        k_c = split_heads(jax.nn.silu(short_conv(pc[..., :qk_w], conv_w[:, qk_w:], conv_b[qk_w:])), dk) * k_scale
        v_c = split_heads(pc[..., qk_w:], dv)
        g_c = gate_terms(u_ctx @ w_in[:, g0:])
        q_c, o_c = None, None

    init = (jnp.zeros((bsz, heads, dv, dk), jnp.float32), jnp.zeros((bsz, heads, dk), jnp.float32),
            jnp.full((bsz, heads), ML_M0, jnp.float32))
    st_f, hc_f = mlstm_chunkwise(k_c, v_c, g_c[0], g_c[1], init, q_c)
    st_b, hc_b = mlstm_chunkwise(_flip(k_c), _flip(v_c), _flip(g_c[2]), _flip(g_c[3]), init, _flip(q_c))
    _, hl_f = mlstm_chunkwise(k_l, v_l, g_l[0], g_l[1], st_f, q_l)
    _, hl_b = mlstm_chunkwise(_flip(k_l), _flip(v_l), _flip(g_l[2]), _flip(g_l[3]), st_b, _flip(q_l))

    def merge(h_f, h_b, o):
        h = h_f + jnp.flip(h_b, axis=1)
        mu = jnp.mean(h, axis=-1, keepdims=True)
        var = jnp.mean(jnp.square(h - mu), axis=-1, keepdims=True)
        hn = ((h - mu) * lax.rsqrt(var + LN_EPS)).reshape(h.shape[0], h.shape[1], v_w) * norm_w
        return (hn.astype(o.dtype) * jax.nn.sigmoid(o)) @ w_out

    y_lat = merge(hl_f, hl_b, o_l)
    y_ctx = merge(hc_f, hc_b, o_c) if with_ctx else None
    return y_lat, y_ctx


def peer_ffn(x, wq, k1, k2, u, v):
    n_tok, d = x.shape
    half = PEER_QDIM // 2
    q = (x @ wq).astype(jnp.float32).reshape(n_tok, PEER_HEADS, PEER_QDIM)
    s1 = jnp.einsum('thd,nd->thn', q[..., :half], k1.astype(jnp.float32))
    s2 = jnp.einsum('thd,nd->thn', q[..., half:], k2.astype(jnp.float32))
    v1, i1 = lax.top_k(s1, PEER_TOPK)
    v2, i2 = lax.top_k(s2, PEER_TOPK)
    cand = (v1[..., :, None] + v2[..., None, :]).reshape(n_tok, PEER_HEADS, PEER_TOPK * PEER_TOPK)
    cand_idx = (i1[..., :, None] * PEER_NKEYS + i2[..., None, :]).reshape(n_tok, PEER_HEADS, PEER_TOPK * PEER_TOPK)
    top_s, top_pos = lax.top_k(cand, PEER_TOPK)
    experts = jnp.take_along_axis(cand_idx, top_pos, axis=-1).reshape(n_tok, PEER_HEADS * PEER_TOPK)
    gates = jax.nn.softmax(top_s, axis=-1).reshape(n_tok, PEER_HEADS * PEER_TOPK).astype(x.dtype)
    n_blk = n_tok // PEER_BLOCK

    def block(args):
        xb, eb, gb = args
        act = jnp.einsum('td,tkd->tk', xb, u[eb])
        return jnp.einsum('tk,tkd->td', gb * jax.nn.gelu(act, approximate=False), v[eb])

    y = lax.map(block, (x.reshape(n_blk, PEER_BLOCK, d),
                        experts.reshape(n_blk, PEER_BLOCK, -1),
                        gates.reshape(n_blk, PEER_BLOCK, -1)))
    return y.reshape(n_tok, d)


def setup_inputs(seed: int = 0) -> dict:
    key = jax.random.key(seed)
    ks = iter(jax.random.split(key, 40))

    def nrm(shape, scale):
        return scale * jax.random.normal(next(ks), shape, jnp.float32)

    d = D_MODEL
    na_w = NA_HEADS * NA_HEAD_DIM
    qk_w, v_w = ML_HEADS * ML_QK_DIM, ML_HEADS * ML_V_DIM
    hid = HY_FILTER_HIDDEN
    fb = jnp.linspace(3.0, 6.0, ML_HEADS, dtype=jnp.float32)
    zb = jnp.zeros_like(fb)
    gate_base = jnp.concatenate([zb, fb, zb, fb])
    return {
        'x': nrm((BATCH, SEQ, d), 1.0),
        'c': nrm((BATCH, d), 1.0),
        'ctx': nrm((BATCH, CTX_LEN, d), 1.0),
        'c_ctx': nrm((d,), 1.0),
        'ada_w': nrm((DEPTH, d, 6 * d), d ** -0.5),
        'ada_b': nrm((DEPTH, 6 * d), 0.02),
        'ln_w': 1.0 + nrm((DEPTH, 2, d), 0.02),
        'ln_b': nrm((DEPTH, 2, d), 0.02),
        'ab_w_in': nrm((N_EVEN, d, 3 * HY_WIDTH + 3 * na_w), d ** -0.5),
        'hy_conv_w': nrm((N_EVEN, HY_SHORT, 3 * HY_WIDTH), HY_SHORT ** -0.5),
        'hy_conv_b': nrm((N_EVEN, 3 * HY_WIDTH), 0.02),
        'hy_fw1': nrm((N_EVEN, HY_EMB, hid), HY_EMB ** -0.5),
        'hy_fb1': nrm((N_EVEN, hid), 0.1),
        'hy_fw2': nrm((N_EVEN, hid, hid), hid ** -0.5),
        'hy_fb2': nrm((N_EVEN, hid), 0.1),
        'hy_fw3': nrm((N_EVEN, hid, hid), hid ** -0.5),
        'hy_fb3': nrm((N_EVEN, hid), 0.1),
        'hy_fw4': nrm((N_EVEN, hid, 2 * HY_ORDER * HY_WIDTH), 0.02 * hid ** -0.5),
        'hy_skip': nrm((N_EVEN, HY_ORDER, HY_WIDTH), 0.5),
        'na_rpb': nrm((N_EVEN, NA_HEADS, 2 * NA_WIN_ROWS - 1, 2 * NA_WIN_COLS - 1), 0.02),
        'ab_w_out': nrm((N_EVEN, HY_WIDTH + na_w, d), DEEPNORM_BETA * (HY_WIDTH + na_w) ** -0.5),
        'ml_w_in': jnp.concatenate([nrm((N_ODD, d, 2 * qk_w + 2 * v_w), d ** -0.5),
                                    nrm((N_ODD, d, 4 * ML_HEADS), 0.1 * d ** -0.5)], axis=-1),
        'ml_b_gates': gate_base[None] + nrm((N_ODD, 4 * ML_HEADS), 0.1),
        'ml_conv_w': nrm((N_ODD, ML_SHORT, 2 * qk_w), ML_SHORT ** -0.5),
        'ml_conv_b': nrm((N_ODD, 2 * qk_w), 0.02),
        'ml_norm_w': 1.0 + nrm((N_ODD, v_w), 0.02),
        'ml_w_out': nrm((N_ODD, v_w, d), DEEPNORM_BETA * v_w ** -0.5),
        'peer_wq': nrm((DEPTH, d, PEER_HEADS * PEER_QDIM), d ** -0.5),
        'peer_k1': nrm((DEPTH, PEER_NKEYS, PEER_QDIM // 2), (PEER_QDIM // 2) ** -0.5),
        'peer_k2': nrm((DEPTH, PEER_NKEYS, PEER_QDIM // 2), (PEER_QDIM // 2) ** -0.5),
        'peer_u': nrm((DEPTH, PEER_EXPERTS, d), d ** -0.5),
        'peer_v': nrm((DEPTH, PEER_EXPERTS, d), DEEPNORM_BETA),
    }


def reference(x, c, ctx, c_ctx, ada_w, ada_b, ln_w, ln_b, ab_w_in, hy_conv_w, hy_conv_b,
              hy_fw1, hy_fb1, hy_fw2, hy_fb2, hy_fw3, hy_fb3, hy_fw4, hy_skip, na_rpb, ab_w_out,
              ml_w_in, ml_b_gates, ml_conv_w, ml_conv_b, ml_norm_w, ml_w_out,
              peer_wq, peer_k1, peer_k2, peer_u, peer_v):
    bsz, seq_len, d = x.shape
    n_ctx = ctx.shape[1]
    h_lat, h_ctx = x, ctx
    for layer in range(DEPTH):
        last = layer == DEPTH - 1
        j = layer // 2
        mod_lat = jnp.split(jax.nn.silu(c) @ ada_w[layer] + ada_b[layer], 6, axis=-1)
        mod_ctx = jnp.split(jax.nn.silu(c_ctx) @ ada_w[layer] + ada_b[layer], 6, axis=-1)
        sh_l, sc_l, g_l, shf_l, scf_l, gf_l = [m[:, None] for m in mod_lat]
        sh_c, sc_c, g_c, shf_c, scf_c, gf_c = mod_ctx
        u_lat = h_lat * (1 + sc_l) + sh_l
        u_ctx = h_ctx * (1 + sc_c) + sh_c
        if layer % 2 == 0:
            y_lat, y_ctx = hyena_na_mixer(u_lat, u_ctx, ab_w_in[j], hy_conv_w[j], hy_conv_b[j],
                                          hy_fw1[j], hy_fb1[j], hy_fw2[j], hy_fb2[j], hy_fw3[j], hy_fb3[j],
                                          hy_fw4[j], hy_skip[j], na_rpb[j], ab_w_out[j], not last)
        else:
            y_lat, y_ctx = mlstm_mixer(u_lat, u_ctx, ml_w_in[j], ml_b_gates[j], ml_conv_w[j], ml_conv_b[j],
                                       ml_norm_w[j], ml_w_out[j], not last)
        h_lat = layer_norm(DEEPNORM_ALPHA * h_lat + g_l * y_lat, ln_w[layer, 0], ln_b[layer, 0])
        v_lat = h_lat * (1 + scf_l) + shf_l
        if last:
            f_lat = peer_ffn(v_lat.reshape(-1, d), peer_wq[layer], peer_k1[layer], peer_k2[layer],
                             peer_u[layer], peer_v[layer]).reshape(bsz, seq_len, d)
        else:
            h_ctx = layer_norm(DEEPNORM_ALPHA * h_ctx + g_c * y_ctx, ln_w[layer, 0], ln_b[layer, 0])
            v_ctx = h_ctx * (1 + scf_c) + shf_c
            f_all = peer_ffn(jnp.concatenate([v_ctx, v_lat], axis=1).reshape(-1, d), peer_wq[layer],
                             peer_k1[layer], peer_k2[layer], peer_u[layer], peer_v[layer])
            f_all = f_all.reshape(bsz, n_ctx + seq_len, d)
            f_lat = f_all[:, n_ctx:]
            h_ctx = layer_norm(DEEPNORM_ALPHA * h_ctx + gf_c * f_all[:, :n_ctx], ln_w[layer, 1], ln_b[layer, 1])
        h_lat = layer_norm(DEEPNORM_ALPHA * h_lat + gf_l * f_lat, ln_w[layer, 1], ln_b[layer, 1])
    return h_lat
```

```python
import functools
import math

import numpy as np
import jax
import jax.numpy as jnp
from jax import lax
from jax.experimental import pallas as pl
from jax.experimental.pallas import tpu as pltpu

_F32 = jnp.float32
_BF16 = jnp.bfloat16

GRID_W = 64
HY_ORDER = 2
HY_EMB = 33
HY_DECAY_TARGET = 1e-2
HY_FAST_PCT = 0.3
HY_SLOW_PCT = 1.5
NA_HEADS = 8
NA_WIN_ROWS = 8
NA_WIN_COLS = 16
ML_HEADS = 8
ML_CHUNK = 64
ML_M0 = -1e30
ROPE_BASE = 10000.0
PEER_HEADS = 8
PEER_NKEYS = 128
PEER_TOPK = 16
LN_EPS = 1e-6

_VMEM_LIMIT = 52 * 1024 * 1024
_NEG = -1e30

_NT = (((1,), (1,)), ((), ()))
_TN = (((0,), (0,)), ((), ()))


def _cp(*sem):
    return pltpu.CompilerParams(dimension_semantics=sem, vmem_limit_bytes=_VMEM_LIMIT)


def _tile(n, cap, align):
    t = (min(cap, n) // align) * align
    while t >= align:
        if n % t == 0:
            return t
        t -= align
    return n


def _split(x):
    hi = x.astype(_BF16)
    lo = (x.astype(_F32) - hi.astype(_F32)).astype(_BF16)
    return hi, lo


def _dot(a, b, passes=1, dims=None):
    dn = dims or (((a.ndim - 1,), (0,)), ((), ()))
    f = lambda x, y: lax.dot_general(x, y, dn, preferred_element_type=_F32)
    if passes == 1:
        return f(a.astype(_BF16), b.astype(_BF16))
    ah, al = _split(a)
    bh, bl = _split(b)
    return f(ah, bh) + (f(ah, bl) + f(al, bh))


def _sigmoid(x):
    return 1.0 / (1.0 + jnp.exp(-x))


def _mm_kernel(*refs, pre, has_bias, passes):
    it = iter(refs)
    a_ref = next(it)
    if pre == "mod":
        sc_ref, sh_ref = next(it), next(it)
    b_ref = next(it)
    bias_ref = next(it) if has_bias else None
    o_ref = next(it)
    a = a_ref[...]
    if pre == "mod":
        a = a.astype(_F32) * (1.0 + sc_ref[...]) + sh_ref[...]
    elif pre == "silu":
        a = a * _sigmoid(a)
    acc = _dot(a, b_ref[...], passes)
    if has_bias:
        acc = acc + bias_ref[...]
    o_ref[...] = acc.astype(o_ref.dtype)


def _mm(a, b, *, scale=None, shift=None, pre=None, bias=None, out_dtype=_F32, passes=1, name="mm"):
    m, k = a.shape
    n = b.shape[1]
    if scale is not None:
        pre = "mod"
    tile_bytes = 4 * 1024 * 1024
    tm = _tile(m, max(8, min(1024, 2 * tile_bytes // (k * a.dtype.itemsize))), 8)
    tn = _tile(n, max(128, min(8192, tile_bytes // (k * b.dtype.itemsize), tile_bytes // (tm * 4))), 128)
    in_specs = [pl.BlockSpec((tm, k), lambda i, j: (i, 0))]
    args = [a]
    if pre == "mod":
        in_specs += [pl.BlockSpec((1, k), lambda i, j: (0, 0))] * 2
        args += [scale, shift]
    in_specs.append(pl.BlockSpec((k, tn), lambda i, j: (0, j)))
    args.append(b)
    if bias is not None:
        in_specs.append(pl.BlockSpec((1, tn), lambda i, j: (0, j)))
        args.append(bias)
    return pl.pallas_call(
        functools.partial(_mm_kernel, pre=pre, has_bias=bias is not None, passes=passes),
        grid=(m // tm, n // tn),
        in_specs=in_specs,
        out_specs=pl.BlockSpec((tm, tn), lambda i, j: (i, j)),
        out_shape=jax.ShapeDtypeStruct((m, n), out_dtype),
        compiler_params=_cp("parallel", "parallel"),
        name=name,
    )(*args)


def _ln_res_kernel(h_ref, y_ref, g_ref, w_ref, b_ref, sc_ref, sh_ref, *o_refs, alpha, with_mod):
    z = alpha * h_ref[...] + g_ref[...] * y_ref[...]
    mu = jnp.mean(z, axis=-1, keepdims=True)
    zc = z - mu
    var = jnp.mean(zc * zc, axis=-1, keepdims=True)
    hn = zc * lax.rsqrt(var + LN_EPS) * w_ref[...] + b_ref[...]
    o_refs[0][...] = hn
    if with_mod:
        o_refs[1][...] = hn * (1.0 + sc_ref[...]) + sh_ref[...]


def _ln_res(h, y, g, w, b, sc, sh, *, alpha, with_mod):
    m, d = h.shape
    tr = _tile(m, 256, 8)
    row = pl.BlockSpec((tr, d), lambda i: (i, 0))
    vec = pl.BlockSpec((1, d), lambda i: (0, 0))
    n_out = 2 if with_mod else 1
    out = pl.pallas_call(
        functools.partial(_ln_res_kernel, alpha=alpha, with_mod=with_mod),
        grid=(m // tr,),
        in_specs=[row, row, vec, vec, vec, vec, vec],
        out_specs=[row] * n_out,
        out_shape=[jax.ShapeDtypeStruct((m, d), _F32)] * n_out,
        compiler_params=_cp("parallel"),
        name="ln_res",
    )(h, y, g, w, b, sc, sh)
    return out if with_mod else (out[0], None)


def _conv3_kernel(*refs, n_out, act, rope, out_scales):
    it = iter(refs)
    xp_ref, x_ref, xn_ref, w_ref, b_ref = (next(it) for _ in range(5))
    if rope:
        cos_ref, sin_ref = next(it), next(it)
    o_refs = [next(it) for _ in range(n_out)]
    i = pl.program_id(0)
    x = x_ref[...]
    tr = x.shape[0]
    rows = lax.broadcasted_iota(jnp.int32, x.shape, 0)
    prev_row = jnp.where(i > 0, xp_ref[7:8, :], 0.0)
    next_row = jnp.where(i < pl.num_programs(0) - 1, xn_ref[0:1, :], 0.0)
    x_m1 = jnp.where(rows == 0, prev_row, pltpu.roll(x, 1, axis=0))
    x_p1 = jnp.where(rows == tr - 1, next_row, pltpu.roll(x, tr - 1, axis=0))
    y = b_ref[...] + x_m1 * w_ref[0:1, :] + x * w_ref[1:2, :] + x_p1 * w_ref[2:3, :]
    if act:
        y = y * _sigmoid(y)
    wo = y.shape[1] // n_out
    for o in range(n_out):
        yo = y[:, o * wo:(o + 1) * wo]
        if rope:
            cos, sin = cos_ref[...], sin_ref[...]
            lane = lax.broadcasted_iota(jnp.int32, cos.shape, 1)
            first = (lane % 64) < 32
            parts = []
            for hh in range(wo // 128):
                yh = yo[:, hh * 128:(hh + 1) * 128]
                sw = jnp.where(first, pltpu.roll(yh, 96, axis=1), pltpu.roll(yh, 32, axis=1))
                parts.append(yh * cos + sw * sin)
            yo = jnp.concatenate(parts, axis=1)
        if out_scales[o] != 1.0:
            yo = yo * out_scales[o]
        o_refs[o][...] = yo.astype(o_refs[o].dtype)


def _conv3(x, w, b, *, n_out, out_dtype, act=False, rope=None, out_scales=None):
    l, cx = x.shape
    tr = _tile(l, 256, 8)
    nb8 = l // 8
    out_scales = out_scales or (1.0,) * n_out
    in_specs = [
        pl.BlockSpec((8, cx), lambda i: (jnp.maximum(i * (tr // 8) - 1, 0), 0)),
        pl.BlockSpec((tr, cx), lambda i: (i, 0)),
        pl.BlockSpec((8, cx), lambda i: (jnp.minimum((i + 1) * (tr // 8), nb8 - 1), 0)),
        pl.BlockSpec((3, cx), lambda i: (0, 0)),
        pl.BlockSpec((1, cx), lambda i: (0, 0)),
    ]
    args = [x, x, x, w, b.reshape(1, cx)]
    if rope is not None:
        in_specs += [pl.BlockSpec((tr, 128), lambda i: (i, 0))] * 2
        args += list(rope)
    wo = cx // n_out
    return pl.pallas_call(
        functools.partial(_conv3_kernel, n_out=n_out, act=act, rope=rope is not None, out_scales=out_scales),
        grid=(l // tr,),
        in_specs=in_specs,
        out_specs=[pl.BlockSpec((tr, wo), lambda i: (i, 0))] * n_out,
        out_shape=[jax.ShapeDtypeStruct((l, wo), out_dtype)] * n_out,
        compiler_params=_cp("parallel"),
        name="conv3",
    )(*args)


def _hy_filter_kernel(f_ref, t_ref, w1_ref, b1_ref, w2_ref, b2_ref, w3_ref, b3_ref, w4_ref, d_ref, o_ref):
    h = jnp.sin(_dot(f_ref[...], w1_ref[...], 3) + b1_ref[...])
    h = jnp.sin(_dot(h, w2_ref[...], 3) + b2_ref[...])
    h = jnp.sin(_dot(h, w3_ref[...], 3) + b3_ref[...])
    taps = _dot(h, w4_ref[...], 3)
    t = t_ref[...]
    window = jnp.exp(-t * d_ref[...])
    c = window.shape[1]
    tl = t.shape[0]
    row = pl.program_id(0) * tl + lax.broadcasted_iota(jnp.int32, (tl, 1), 0)
    not_first = jnp.where(row == 0, 0.0, 1.0)
    n_grp = taps.shape[1] // c
    for g in range(n_grp):
        blk = taps[:, g * c:(g + 1) * c] * window
        if g >= n_grp // 2:
            blk = blk * not_first
        o_ref[:, g * c:(g + 1) * c] = blk.astype(o_ref.dtype)


def _hy_filter(seq_len, fw1, fb1, fw2, fb2, fw3, fb3, fw4, width):
    hid = fw1.shape[1]
    t = jnp.linspace(0.0, 1.0, seq_len, dtype=_F32)[:, None]
    omega = 2.0 * math.pi * jnp.arange(seq_len, dtype=_F32) / seq_len
    bands = (HY_EMB - 1) // 2
    freqs = jnp.linspace(1e-4, bands - 1, bands, dtype=_F32)
    ang = omega[:, None] * freqs[None, :]
    kpad = 64
    feats = jnp.concatenate([t, jnp.cos(ang), -jnp.sin(ang), jnp.zeros((seq_len, kpad - HY_EMB), _F32)], axis=-1)
    w1 = jnp.concatenate([fw1, jnp.zeros((kpad - HY_EMB, hid), _F32)], axis=0)
    max_decay = math.log(HY_DECAY_TARGET) / HY_FAST_PCT
    min_decay = math.log(HY_DECAY_TARGET) / HY_SLOW_PCT
    deltas = jnp.abs(jnp.linspace(min_decay, max_decay, width, dtype=_F32))[None, :]
    n_cols = fw4.shape[1]
    tl = _tile(seq_len, 256, 8)
    full = lambda shape: pl.BlockSpec(shape, lambda i: (0,) * len(shape))
    return pl.pallas_call(
        _hy_filter_kernel,
        grid=(seq_len // tl,),
        in_specs=[pl.BlockSpec((tl, kpad), lambda i: (i, 0)), pl.BlockSpec((tl, 1), lambda i: (i, 0)),
                  full((kpad, hid)), full((1, hid)), full((hid, hid)), full((1, hid)), full((hid, hid)), full((1, hid)),
                  full((hid, n_cols)), full((1, width))],
        out_specs=pl.BlockSpec((tl, n_cols), lambda i: (i, 0)),
        out_shape=jax.ShapeDtypeStruct((seq_len, n_cols), _BF16),
        compiler_params=_cp("parallel"),
        name="hy_filter",
    )(feats, t, w1, fb1.reshape(1, hid), fw2, fb2.reshape(1, hid), fw3, fb3.reshape(1, hid), fw4, deltas)


def _fft_consts(n1, n2):
    n = n1 * n2
    k1 = np.arange(n1)[:, None]
    j1 = np.arange(n1 // 2)[None, :]
    ang1 = 2.0 * np.pi * k1 * j1 / n1
    f1_fwd = np.concatenate([np.cos(ang1), -np.sin(ang1)], axis=0)
    f1_inv = np.concatenate([np.cos(ang1).T, -np.sin(ang1).T], axis=1) / n
    k2 = np.arange(n2)[:, None]
    j2 = np.arange(n2)[None, :]
    ang2 = 2.0 * np.pi * k2 * j2 / n2
    f2 = np.stack([np.cos(ang2), -np.sin(ang2)])
    angt = 2.0 * np.pi * np.arange(n1)[:, None] * np.arange(n2)[None, :] / n
    tw = np.stack([np.cos(angt), -np.sin(angt)], axis=1)
    as32 = lambda a: jnp.asarray(a, dtype=_F32)
    return as32(f1_fwd), as32(f1_inv), as32(f2), as32(tw), as32(np.transpose(tw, (0, 2, 1)))


def _cmm(f_stack, xr, xi):
    n = f_stack.shape[0] // 2
    p1 = jnp.dot(f_stack, xr, preferred_element_type=_F32)
    p2 = jnp.dot(f_stack, xi, preferred_element_type=_F32)
    return p1[:n] - p2[n:], p2[:n] + p1[n:]


def _twiddled_dft(f2_ref, tw_ref):
    f2r, f2i = f2_ref[0], f2_ref[1]
    tr, ti = tw_ref[0, 0:1, :], tw_ref[0, 1:2, :]
    return jnp.concatenate([f2r * tr - f2i * ti, f2r * ti + f2i * tr], axis=0).astype(_BF16)


def _fft_spec_kernel(af_ref, ab_ref, f2_ref, tw_ref, o_ref):
    fw = _twiddled_dft(f2_ref, tw_ref)
    fr, fi = _cmm(fw, af_ref[0, 0], af_ref[1, 0])
    br, bi = _cmm(fw, ab_ref[0, 0], ab_ref[1, 0])
    o_ref[0, 0, 0] = fr + br
    o_ref[0, 1, 0] = fi - bi


def _fft_mid_kernel(a_ref, h_ref, f2_ref, tw_ref, twc_ref, o_ref):
    fw = _twiddled_dft(f2_ref, tw_ref)
    br, bi = _cmm(fw, a_ref[0, 0], a_ref[1, 0])
    hr, hi = h_ref[0, 0, 0], h_ref[0, 1, 0]
    yr = (br * hr - bi * hi).astype(_BF16)
    yi = (br * hi + bi * hr).astype(_BF16)
    f2r, f2i = f2_ref[0], f2_ref[1]
    tcr, tci = twc_ref[0, :, 0:1], twc_ref[0, :, 1:2]
    g = jnp.concatenate([tcr * f2r - tci * f2i, -(tcr * f2i + tci * f2r)], axis=0).astype(_BF16)
    zr, zi = _cmm(g, yr, yi)
    o_ref[0, 0] = zr.astype(o_ref.dtype)
    o_ref[1, 0] = zi.astype(o_ref.dtype)


def _fft_shape(l_pad):
    n2 = 256 if l_pad >= 8192 else 64
    n1 = 2 * l_pad // n2
    return n1, n2


def _hy_spectra(filt, width, consts, n1, n2):
    f1_fwd, _, f2, tw, _ = consts
    n_cols = filt.shape[1]
    a = _mm(f1_fwd, filt.reshape(n1 // 2, n2 * n_cols), out_dtype=_BF16, name="fft_a_filt")
    a = a.reshape(2, n1, n2, n_cols)
    blk = lambda d: pl.BlockSpec((2, 1, n2, width), lambda k, o: (0, k, 0, d * HY_ORDER + o))
    return pl.pallas_call(
        _fft_spec_kernel,
        grid=(n1, HY_ORDER),
        in_specs=[blk(0), blk(1), pl.BlockSpec((2, n2, n2), lambda k, o: (0, 0, 0)),
                  pl.BlockSpec((1, 2, n2), lambda k, o: (k, 0, 0))],
        out_specs=pl.BlockSpec((1, 2, 1, n2, width), lambda k, o: (o, 0, k, 0, 0)),
        out_shape=jax.ShapeDtypeStruct((HY_ORDER, 2, n1, n2, width), _F32),
        compiler_params=_cp("parallel", "parallel"),
        name="fft_spec",
    )(a, a, f2, tw)


def _long_conv(z, spec, order, consts, n1, n2):
    f1_fwd, f1_inv, f2, tw, twc = consts
    l_pad, c = z.shape
    a = _mm(f1_fwd, z.reshape(n1 // 2, n2 * c), out_dtype=_BF16, name="fft_a")
    a = a.reshape(2, n1, n2, c)
    blk = pl.BlockSpec((2, 1, n2, c), lambda k: (0, k, 0, 0))
    mid = pl.pallas_call(
        _fft_mid_kernel,
        grid=(n1,),
        in_specs=[blk, pl.BlockSpec((1, 2, 1, n2, c), lambda k: (order, 0, k, 0, 0)),
                  pl.BlockSpec((2, n2, n2), lambda k: (0, 0, 0)),
                  pl.BlockSpec((1, 2, n2), lambda k: (k, 0, 0)), pl.BlockSpec((1, n2, 2), lambda k: (k, 0, 0))],
        out_specs=blk,
        out_shape=jax.ShapeDtypeStruct((2, n1, n2, c), _BF16),
        compiler_params=_cp("parallel"),
        name="fft_mid",
    )(a, spec, f2, tw, twc)
    y = _mm(f1_inv, mid.reshape(2 * n1, n2 * c), name="fft_c")
    return y.reshape(l_pad, c)


def _hy_gate_kernel(y_ref, z_ref, s_ref, g_ref, o_ref):
    o_ref[...] = (g_ref[...] * (y_ref[...] + z_ref[...] * s_ref[...])).astype(o_ref.dtype)


def _hy_gate(y, z, skip, gate, out_dtype):
    l, c = y.shape
    tr = _tile(l, 512, 8)
    row = pl.BlockSpec((tr, c), lambda i: (i, 0))
    return pl.pallas_call(
        _hy_gate_kernel,
        grid=(l // tr,),
        in_specs=[row, row, pl.BlockSpec((1, c), lambda i: (0, 0)), row],
        out_specs=row,
        out_shape=jax.ShapeDtypeStruct((l, c), out_dtype),
        compiler_params=_cp("parallel"),
        name="hy_gate",
    )(y, z, skip.reshape(1, c), gate)


def _hyena(p, conv_w, conv_b, fparams, skip):
    l = p.shape[0]
    width = p.shape[1] // (HY_ORDER + 1)
    l_pad = max(l, 1024)
    n1, n2 = _fft_shape(l_pad)
    consts = _fft_consts(n1, n2)
    filt = _hy_filter(l, *fparams, width)
    parts = _conv3(p, conv_w, conv_b, n_out=HY_ORDER + 1, out_dtype=_F32)
    if l_pad != l:
        pad = lambda a: jnp.pad(a, ((0, l_pad - l), (0, 0)))
        filt = pad(filt)
        parts = [pad(a) for a in parts]
    spec = _hy_spectra(filt, width, consts, n1, n2)
    z = parts[0]
    for o in range(HY_ORDER):
        y = _long_conv(z, spec, o, consts, n1, n2)
        z = _hy_gate(y, z, skip[o], parts[o + 1], _F32 if o + 1 < HY_ORDER else _BF16)
    return z[:l]


def _na_bias_table(rpb, w):
    kr, kc = NA_WIN_ROWS, NA_WIN_COLS
    cols = np.arange(w)
    start = np.clip(cols - kc // 2, 0, w - kc)
    kcol = np.arange(w)[None, :]
    in_win = (kcol >= start[:, None]) & (kcol < start[:, None] + kc)
    col_off = np.clip(kcol - cols[:, None] + kc - 1, 0, 2 * kc - 2)
    d = np.arange(kr)[:, None]
    i = np.arange(kr)[None, :]
    row_off = i - d + kr - 1
    t = rpb[:, row_off[:, :, None, None], col_off[None, None, :, :]]
    t = jnp.where(in_win[None, None, None], t, _NEG)
    t = jnp.transpose(t, (0, 1, 3, 2, 4))
    return t.reshape(rpb.shape[0], kr, w, kr * w)


def _natten_kernel(q_ref, k_ref, v_ref, kc_ref, vc_ref, bias_ref, o_ref, *, rows, w, scale):
    qb = pl.program_id(1)
    kr = NA_WIN_ROWS
    kc, vc = kc_ref[...], vc_ref[...]
    for a in range(kr):
        r = qb * kr + a
        r0 = jnp.clip(r - kr // 2, 0, rows - kr)
        d = r - r0
        start = pl.multiple_of(r0 * w, w)
        kwin = k_ref[pl.ds(start, kr * w), :]
        vwin = v_ref[pl.ds(start, kr * w), :]
        qa = q_ref[a * w:(a + 1) * w, :]
        s_lat = _dot(qa, kwin, dims=_NT) * scale + bias_ref[0, d]
        s_ctx = _dot(qa, kc, dims=_NT) * scale
        m = jnp.maximum(jnp.max(s_lat, axis=-1, keepdims=True), jnp.max(s_ctx, axis=-1, keepdims=True))
        p_lat = jnp.exp(s_lat - m)
        p_ctx = jnp.exp(s_ctx - m)
        den = jnp.sum(p_lat, axis=-1, keepdims=True) + jnp.sum(p_ctx, axis=-1, keepdims=True)
        o = _dot(p_lat, vwin) + _dot(p_ctx, vc)
        o_ref[a * w:(a + 1) * w, :] = (o / den).astype(o_ref.dtype)


def _natten(qkv, kv_ctx, rpb):
    l = qkv.shape[0]
    h = NA_HEADS
    hd = qkv.shape[1] // (3 * h)
    w = GRID_W
    rows = l // w
    kr = NA_WIN_ROWS
    assert rows % kr == 0 and rows >= kr and hd % 128 == 0
    lc = kv_ctx.shape[0]
    bias = _na_bias_table(rpb, w)
    tq = kr * w
    return pl.pallas_call(
        functools.partial(_natten_kernel, rows=rows, w=w, scale=hd ** -0.5),
        grid=(h, rows // kr),
        in_specs=[pl.BlockSpec((tq, hd), lambda hh, qb: (qb, hh)),
                  pl.BlockSpec((l, hd), lambda hh, qb: (0, h + hh)),
                  pl.BlockSpec((l, hd), lambda hh, qb: (0, 2 * h + hh)),
                  pl.BlockSpec((lc, hd), lambda hh, qb: (0, h + hh)),
                  pl.BlockSpec((lc, hd), lambda hh, qb: (0, 2 * h + hh)),
                  pl.BlockSpec((1, kr, w, kr * w), lambda hh, qb: (hh, 0, 0, 0))],
        out_specs=pl.BlockSpec((tq, hd), lambda hh, qb: (qb, hh)),
        out_shape=jax.ShapeDtypeStruct((l, h * hd), _BF16),
        compiler_params=_cp("parallel", "parallel"),
        name="natten",
    )(qkv, qkv, qkv, kv_ctx, kv_ctx, bias)


def _ctx_attn_kernel(q_ref, k_ref, v_ref, o_ref, *, scale):
    s = _dot(q_ref[...], k_ref[...], dims=_NT) * scale
    p = jnp.exp(s - jnp.max(s, axis=-1, keepdims=True))
    den = jnp.sum(p, axis=-1, keepdims=True)
    o_ref[...] = (_dot(p, v_ref[...]) / den).astype(o_ref.dtype)


def _ctx_attn(qkv):
    lc = qkv.shape[0]
    h = NA_HEADS
    hd = qkv.shape[1] // (3 * h)
    blk = lambda off: pl.BlockSpec((lc, hd), lambda hh: (0, off + hh))
    return pl.pallas_call(
        functools.partial(_ctx_attn_kernel, scale=hd ** -0.5),
        grid=(h,),
        in_specs=[blk(0), blk(h), blk(2 * h)],
        out_specs=blk(0),
        out_shape=jax.ShapeDtypeStruct((lc, h * hd), _BF16),
        compiler_params=_cp("parallel"),
        name="ctx_attn",
    )(qkv, qkv, qkv)


def _ml_gates_kernel(g_ref, b_ref, o_ref, *, heads):
    g = g_ref[...] + b_ref[...]
    logsig = jnp.minimum(g, 0.0) - jnp.log(1.0 + jnp.exp(-jnp.abs(g)))
    lane = lax.broadcasted_iota(jnp.int32, g.shape, 1)
    is_forget = (lane // heads) % 2 == 1
    o_ref[...] = jnp.where(is_forget, logsig, g)


def _ml_gates(g_raw, b_gates):
    l, n = g_raw.shape
    tr = _tile(l, 1024, 8)
    bias = jnp.pad(b_gates, (0, n - b_gates.shape[0])).reshape(1, n)
    return pl.pallas_call(
        functools.partial(_ml_gates_kernel, heads=ML_HEADS),
        grid=(l // tr,),
        in_specs=[pl.BlockSpec((tr, n), lambda i: (i, 0)), pl.BlockSpec((1, n), lambda i: (0, 0))],
        out_specs=pl.BlockSpec((tr, n), lambda i: (i, 0)),
        out_shape=jax.ShapeDtypeStruct((l, n), _F32),
        compiler_params=_cp("parallel"),
        name="ml_gates",
    )(g_raw, bias)


def _ml_scan_kernel(*refs, heads, dk, dv, reverse, with_h):
    it = iter(refs)
    if with_h:
        q_ref = next(it)
    k_ref, v_ref, g_ref, gt_ref, c0_ref, n0_ref, m0_ref = (next(it) for _ in range(7))
    if with_h:
        h_ref = next(it)
    c_ref, n_ref, m_ref = next(it), next(it), next(it)

    @pl.when(pl.program_id(0) == 0)
    def _():
        c_ref[...] = c0_ref[...]
        n_ref[...] = n0_ref[...]
        m_ref[...] = m0_ref[...]

    t = k_ref.shape[0]
    ti = lax.broadcasted_iota(jnp.int32, (t, t), 0)
    si = lax.broadcasted_iota(jnp.int32, (t, t), 1)
    seen = (si >= ti) if reverse else (si <= ti)
    seen_t = (ti >= si) if reverse else (ti <= si)
    g = g_ref[...]
    gt = gt_ref[0]
    c0 = 2 * heads if reverse else 0
    for h in range(heads):
        i_col = g[:, c0 + h:c0 + h + 1]
        f_col = g[:, c0 + heads + h:c0 + heads + h + 1]
        i_row = gt[c0 + h:c0 + h + 1, :]
        f_row = gt[c0 + heads + h:c0 + heads + h + 1, :]
        b_col = jnp.sum(jnp.where(seen, f_row, 0.0), axis=1, keepdims=True)
        b_row = jnp.sum(jnp.where(seen_t, f_col, 0.0), axis=0, keepdims=True)
        b_end = jnp.sum(f_row, axis=1, keepdims=True)
        m_prev = m_ref[h][:, 0:1]
        w_log_row = b_end - b_row + i_row
        w_log_col = b_end - b_col + i_col
        m_new = jnp.maximum(b_end + m_prev, jnp.max(w_log_row, axis=1, keepdims=True))
        decay = jnp.exp(b_end + m_prev - m_new)
        w_col = jnp.exp(w_log_col - m_new)
        kh = k_ref[:, h * dk:(h + 1) * dk]
        vh = v_ref[:, h * dv:(h + 1) * dv]
        ct = c_ref[h]
        nv = n_ref[h]
        if with_h:
            qh = q_ref[:, h * dk:(h + 1) * dk]
            log_d = jnp.where(seen, b_col - b_row + i_row, -jnp.inf)
            m_in = b_col + m_prev
            m_t = jnp.maximum(m_in, jnp.max(log_d, axis=1, keepdims=True))
            sc = _dot(qh, kh, dims=_NT) * jnp.exp(log_d - m_t)
            g_in = jnp.exp(m_in - m_t)
            num = _dot(sc, vh) + _dot(qh, ct) * g_in
            den = jnp.sum(sc, axis=1, keepdims=True) + jnp.sum(qh.astype(_F32) * nv, axis=1, keepdims=True) * g_in
            h_ref[:, h * dv:(h + 1) * dv] = num / jnp.maximum(jnp.abs(den), jnp.exp(-m_t))
        vw = vh.astype(_F32) * w_col
        c_ref[h] = decay * ct + _dot(kh, vw, dims=_TN)
        n_ref[h] = decay * nv + jnp.sum(kh.astype(_F32) * w_col, axis=0, keepdims=True)
        m_ref[h] = jnp.broadcast_to(m_new, m_ref.shape[1:])


def _ml_scan(q, k, v, gates, state, *, reverse, dk, dv):
    l = k.shape[0]
    heads = ML_HEADS
    t = ML_CHUNK
    nc = l // t
    with_h = q is not None
    gt = jnp.transpose(gates.reshape(nc, t, gates.shape[1]), (0, 2, 1))
    cidx = (lambda j: nc - 1 - j) if reverse else (lambda j: j)
    row = lambda width: pl.BlockSpec((t, width), lambda j: (cidx(j), 0))
    st_specs = [pl.BlockSpec((heads, dk, dv), lambda j: (0, 0, 0)), pl.BlockSpec((heads, 1, dk), lambda j: (0, 0, 0)),
                pl.BlockSpec((heads, 1, 128), lambda j: (0, 0, 0))]
    st_shapes = [jax.ShapeDtypeStruct((heads, dk, dv), _F32), jax.ShapeDtypeStruct((heads, 1, dk), _F32),
                 jax.ShapeDtypeStruct((heads, 1, 128), _F32)]
    in_specs = ([row(heads * dk)] if with_h else []) + [
        row(heads * dk), row(heads * dv), row(gates.shape[1]),
        pl.BlockSpec((1, gates.shape[1], t), lambda j: (cidx(j), 0, 0))] + st_specs
    args = ([q] if with_h else []) + [k, v, gates, gt, *state]
    out_specs = ([row(heads * dv)] if with_h else []) + st_specs
    out_shape = ([jax.ShapeDtypeStruct((l, heads * dv), _F32)] if with_h else []) + st_shapes
    out = pl.pallas_call(
        functools.partial(_ml_scan_kernel, heads=heads, dk=dk, dv=dv, reverse=reverse, with_h=with_h),
        grid=(nc,),
        in_specs=in_specs,
        out_specs=out_specs,
        out_shape=out_shape,
        compiler_params=_cp("arbitrary"),
        name="ml_scan",
    )(*args)
    return (out[0], tuple(out[1:])) if with_h else (None, tuple(out))


def _ml_merge_kernel(hf_ref, hb_ref, o_ref, w_ref, out_ref, *, heads, dv):
    for h in range(heads):
        sl = slice(h * dv, (h + 1) * dv)
        x = hf_ref[:, sl] + hb_ref[:, sl]
        mu = jnp.mean(x, axis=-1, keepdims=True)
        xc = x - mu
        var = jnp.mean(xc * xc, axis=-1, keepdims=True)
        hn = xc * lax.rsqrt(var + LN_EPS) * w_ref[:, sl]
        out_ref[:, sl] = (hn * _sigmoid(o_ref[:, sl])).astype(out_ref.dtype)


def _ml_merge(hf, hb, o, norm_w, dv):
    l, n = hf.shape
    tr = _tile(l, 256, 8)
    row = pl.BlockSpec((tr, n), lambda i: (i, 0))
    return pl.pallas_call(
        functools.partial(_ml_merge_kernel, heads=ML_HEADS, dv=dv),
        grid=(l // tr,),
        in_specs=[row, row, row, pl.BlockSpec((1, n), lambda i: (0, 0))],
        out_specs=row,
        out_shape=jax.ShapeDtypeStruct((l, n), _BF16),
        compiler_params=_cp("parallel"),
        name="ml_merge",
    )(hf, hb, o, norm_w.reshape(1, n))


def _rope_tables(l):
    quarter = 32
    pos = jnp.arange(l)
    inv = ROPE_BASE ** (-jnp.arange(quarter, dtype=_F32) / quarter)
    ang_r = (pos // GRID_W).astype(_F32)[:, None] * inv[None]
    ang_c = (pos % GRID_W).astype(_F32)[:, None] * inv[None]
    cos = jnp.concatenate([jnp.cos(ang_r)] * 2 + [jnp.cos(ang_c)] * 2, axis=-1)
    sin = jnp.concatenate([-jnp.sin(ang_r), jnp.sin(ang_r), -jnp.sin(ang_c), jnp.sin(ang_c)], axis=-1)
    return cos, sin


def _mlstm_mixer(h_lat, h_ctx, mod_l, mod_c, w_in, b_gates, conv_w, conv_b, norm_w, w_out):
    heads = ML_HEADS
    v_w = w_out.shape[0]
    dv = v_w // heads
    dk = dv // 2
    qk_w = heads * dk
    g0 = 2 * qk_w + 2 * v_w
    n_g = w_in.shape[1] - g0
    assert dk == 128
    l = h_lat.shape[0]
    wb = w_in.astype(_BF16)
    w_g = jnp.pad(w_in[:, g0:], ((0, 0), (0, 128 - n_g)))
    sc_l, sh_l = mod_l
    sc_c, sh_c = mod_c
    qk_l = _mm(h_lat, wb[:, :2 * qk_w], scale=sc_l, shift=sh_l, name="ml_proj_qk")
    v_l = _mm(h_lat, wb[:, 2 * qk_w:2 * qk_w + v_w], scale=sc_l, shift=sh_l, out_dtype=_BF16, name="ml_proj_v")
    o_l = _mm(h_lat, wb[:, 2 * qk_w + v_w:g0], scale=sc_l, shift=sh_l, name="ml_proj_o")
    g_l = _ml_gates(_mm(h_lat, w_g, scale=sc_l, shift=sh_l, passes=3, name="ml_proj_g"), b_gates)
    k_c_raw = _mm(h_ctx, wb[:, qk_w:2 * qk_w], scale=sc_c, shift=sh_c, name="ml_proj_kc")
    v_c = _mm(h_ctx, wb[:, 2 * qk_w:2 * qk_w + v_w], scale=sc_c, shift=sh_c, out_dtype=_BF16, name="ml_proj_vc")
    g_c = _ml_gates(_mm(h_ctx, w_g, scale=sc_c, shift=sh_c, passes=3, name="ml_proj_gc"), b_gates)
    k_scale = dk ** -0.5
    q_l, k_l = _conv3(qk_l, conv_w, conv_b, n_out=2, out_dtype=_BF16, act=True, rope=_rope_tables(l),
                      out_scales=(1.0, k_scale))
    (k_c,) = _conv3(k_c_raw, conv_w[:, qk_w:], conv_b[qk_w:], n_out=1, out_dtype=_BF16, act=True,
                    out_scales=(k_scale,))
    init = (jnp.zeros((heads, dk, dv), _F32), jnp.zeros((heads, 1, dk), _F32), jnp.full((heads, 1, 128), ML_M0, _F32))
    scan = functools.partial(_ml_scan, dk=dk, dv=dv)
    _, st_f = scan(None, k_c, v_c, g_c, init, reverse=False)
    _, st_b = scan(None, k_c, v_c, g_c, init, reverse=True)
    hl_f, _ = scan(q_l, k_l, v_l, g_l, st_f, reverse=False)
    hl_b, _ = scan(q_l, k_l, v_l, g_l, st_b, reverse=True)
    merged = _ml_merge(hl_f, hl_b, o_l, norm_w, dv)
    return _mm(merged, w_out.astype(_BF16), name="ml_out")


def _top_desc(x, count):
    r = x.shape[0]
    idx = lax.broadcasted_iota(jnp.int32, x.shape, 0).astype(_F32)
    vals = []
    for _ in range(count):
        m = jnp.max(x, axis=0, keepdims=True)
        vals.append(m)
        first = jnp.min(jnp.where(x == m, idx, float(r)), axis=0, keepdims=True)
        x = jnp.where(idx == first, -jnp.inf, x)
    return vals


def _peer_route_kernel(q_ref, k1_ref, k2_ref, e1_ref, e2_ref, th_ref):
    half = k1_ref.shape[1]
    q = q_ref[...]
    s1 = _dot(k1_ref[...], q[:, :half], 3, dims=_NT)
    s2 = _dot(k2_ref[...], q[:, half:], 3, dims=_NT)
    n_top = PEER_TOPK + 1
    v1 = _top_desc(s1, n_top)
    v2 = _top_desc(s2, n_top)
    cands = [v1[a] + v2[b] for a in range(n_top) for b in range(n_top) if (a + 1) * (b + 1) <= n_top]
    pad = (-len(cands)) % 8
    cand = jnp.concatenate(cands + [jnp.full_like(cands[0], -jnp.inf)] * pad, axis=0)
    top = _top_desc(cand, n_top)
    m = top[0]
    z = top[0] * 0.0
    for j in range(PEER_TOPK):
        z = z + jnp.exp(top[j] - m)
    tau = 0.5 * (top[PEER_TOPK - 1] + top[PEER_TOPK])
    inv_z = 1.0 / z
    e1_ref[0] = jnp.exp(s1 - v1[0])
    e2_ref[0] = jnp.exp(s2 - v2[0]) * inv_z
    th_ref[0] = jnp.exp(tau - m) * inv_z


def _peer_dense_kernel(x_ref, u_ref, vt_ref, e1_ref, e2_ref, th_ref, o_ref, a_ref, *, heads, nkeys):
    @pl.when(pl.program_id(1) == 0)
    def _():
        o_ref[...] = jnp.zeros_like(o_ref)

    x = x_ref[...]
    n_sub = u_ref.shape[0] // nkeys
    for i in range(n_sub):
        act = _dot(u_ref[i * nkeys:(i + 1) * nkeys, :], x, dims=_NT)
        gate = jnp.zeros_like(act)
        for h in range(heads):
            p = e1_ref[h, i:i + 1, :] * e2_ref[h]
            gate = gate + jnp.where(p >= th_ref[h], p, 0.0)
        gelu = 0.5 * act * (1.0 + lax.erf(act * (2.0 ** -0.5)))
        a_ref[i * nkeys:(i + 1) * nkeys, :] = (gate * gelu).astype(a_ref.dtype)
    o_ref[...] += jnp.dot(vt_ref[...], a_ref[...], preferred_element_type=_F32)


def _peer(x, wq, k1, k2, u_bf, vt_bf):
    t, d = x.shape
    heads, nkeys = PEER_HEADS, PEER_NKEYS
    qdim = wq.shape[1] // heads
    q = _mm(x, wq, passes=3, name="peer_q")
    tt = _tile(t, 512, 128)
    e_shape = jax.ShapeDtypeStruct((heads, nkeys, t), _F32)
    e_blk = pl.BlockSpec((1, nkeys, tt), lambda i, h: (h, 0, i))
    th_blk = pl.BlockSpec((1, 1, tt), lambda i, h: (h, 0, i))
    e1, e2, th = pl.pallas_call(
        _peer_route_kernel,
        grid=(t // tt, heads),
        in_specs=[pl.BlockSpec((tt, qdim), lambda i, h: (i, h)),
                  pl.BlockSpec((nkeys, qdim // 2), lambda i, h: (0, 0)),
                  pl.BlockSpec((nkeys, qdim // 2), lambda i, h: (0, 0))],
        out_specs=[e_blk, e_blk, th_blk],
        out_shape=[e_shape, e_shape, jax.ShapeDtypeStruct((heads, 1, t), _F32)],
        compiler_params=_cp("parallel", "parallel"),
        name="peer_route",
    )(q, k1, k2)
    tb = _tile(t, 512, 128)
    n_sub = 8
    eb = n_sub * nkeys
    xb = x.astype(_BF16)
    return pl.pallas_call(
        functools.partial(_peer_dense_kernel, heads=heads, nkeys=nkeys),
        grid=(t // tb, nkeys // n_sub),
        in_specs=[pl.BlockSpec((tb, d), lambda i, e: (i, 0)),
                  pl.BlockSpec((eb, d), lambda i, e: (e, 0)),
                  pl.BlockSpec((d, eb), lambda i, e: (0, e)),
                  pl.BlockSpec((heads, n_sub, tb), lambda i, e: (0, e, i)),
                  pl.BlockSpec((heads, nkeys, tb), lambda i, e: (0, 0, i)),
                  pl.BlockSpec((heads, 1, tb), lambda i, e: (0, 0, i))],
        out_specs=pl.BlockSpec((d, tb), lambda i, e: (0, i)),
        out_shape=jax.ShapeDtypeStruct((d, t), _F32),
        scratch_shapes=[pltpu.VMEM((eb, tb), _BF16)],
        compiler_params=_cp("parallel", "arbitrary"),
        name="peer_dense",
    )(xb, u_bf, vt_bf, e1, e2, th)


def _hyena_na_mixer(h_lat, h_ctx, mod_l, mod_c, w_in, conv_w, conv_b, fparams, skip, rpb, w_out):
    split = conv_w.shape[1]
    wb = w_in.astype(_BF16)
    w_hy, w_na = wb[:, :split], wb[:, split:]
    outs = []
    qkvs = []
    for hh, (sc, sh) in ((h_lat, mod_l), (h_ctx, mod_c)):
        p_hy = _mm(hh, w_hy, scale=sc, shift=sh, name="hy_proj")
        qkvs.append(_mm(hh, w_na, scale=sc, shift=sh, out_dtype=_BF16, name="na_proj"))
        outs.append(_hyena(p_hy, conv_w, conv_b, fparams, skip))
    na_lat = _natten(qkvs[0], qkvs[1], rpb)
    na_ctx = _ctx_attn(qkvs[1])
    wo = w_out.astype(_BF16)
    y_lat = _mm(jnp.concatenate([outs[0], na_lat], axis=-1), wo, name="ab_out")
    y_ctx = _mm(jnp.concatenate([outs[1], na_ctx], axis=-1), wo, name="ab_out")
    return y_lat, y_ctx


def kernel(x, c, ctx, c_ctx, ada_w, ada_b, ln_w, ln_b, ab_w_in, hy_conv_w, hy_conv_b, hy_fw1, hy_fb1, hy_fw2, hy_fb2, hy_fw3, hy_fb3, hy_fw4, hy_skip, na_rpb, ab_w_out, ml_w_in, ml_b_gates, ml_conv_w, ml_conv_b, ml_norm_w, ml_w_out, peer_wq, peer_k1, peer_k2, peer_u, peer_v):
    bsz, seq_len, d = x.shape
    depth = ada_w.shape[0]
    assert bsz == 1 and depth == 2, "one even (Hyena/attention) layer followed by one last odd (mLSTM) layer"
    alpha = (2 * depth) ** 0.25
    h_lat, h_ctx = x[0], ctx[0]
    cvec = jnp.zeros((8, d), _F32).at[0].set(c[0]).at[1].set(c_ctx)
    row = lambda a: a.reshape(1, d)
    for layer in range(depth):
        last = layer == depth - 1
        mod = _mm(cvec, ada_w[layer], pre="silu", bias=ada_b[layer].reshape(1, 6 * d), passes=3, name="ada_mod")
        sh_l, sc_l, g_l, shf_l, scf_l, gf_l = [mod[0:1, i * d:(i + 1) * d] for i in range(6)]
        sh_c, sc_c, g_c, shf_c, scf_c, gf_c = [mod[1:2, i * d:(i + 1) * d] for i in range(6)]
        if layer % 2 == 0:
            fparams = (hy_fw1[0], hy_fb1[0], hy_fw2[0], hy_fb2[0], hy_fw3[0], hy_fb3[0], hy_fw4[0])
            y_lat, y_ctx = _hyena_na_mixer(h_lat, h_ctx, (sc_l, sh_l), (sc_c, sh_c), ab_w_in[0], hy_conv_w[0],
                                           hy_conv_b[0], fparams, hy_skip[0], na_rpb[0], ab_w_out[0])
        else:
            y_lat = _mlstm_mixer(h_lat, h_ctx, (sc_l, sh_l), (sc_c, sh_c), ml_w_in[0], ml_b_gates[0], ml_conv_w[0],
                                 ml_conv_b[0], ml_norm_w[0], ml_w_out[0])
            y_ctx = None
        lw0, lb0, lw1, lb1 = row(ln_w[layer, 0]), row(ln_b[layer, 0]), row(ln_w[layer, 1]), row(ln_b[layer, 1])
        u_bf = peer_u[layer].astype(_BF16)
        vt_bf = peer_v[layer].astype(_BF16).T
        peer = lambda v: _peer(v, peer_wq[layer], peer_k1[layer], peer_k2[layer], u_bf, vt_bf).T
        h_lat, v_lat = _ln_res(h_lat, y_lat, g_l, lw0, lb0, scf_l, shf_l, alpha=alpha, with_mod=True)
        h_lat, _ = _ln_res(h_lat, peer(v_lat), gf_l, lw1, lb1, scf_l, shf_l, alpha=alpha, with_mod=False)
        if not last:
            h_ctx, v_ctx = _ln_res(h_ctx, y_ctx, g_c, lw0, lb0, scf_c, shf_c, alpha=alpha, with_mod=True)
            h_ctx, _ = _ln_res(h_ctx, peer(v_ctx), gf_c, lw1, lb1, scf_c, shf_c, alpha=alpha, with_mod=False)
    return h_lat[None]
```

```python
import functools
import math

import numpy as np
import jax
import jax.numpy as jnp
from jax import lax
from jax.experimental import pallas as pl
from jax.experimental.pallas import tpu as pltpu

_F32 = jnp.float32
_BF16 = jnp.bfloat16

GRID_W = 64
HY_ORDER = 2
HY_EMB = 33
HY_DECAY_TARGET = 1e-2
HY_FAST_PCT = 0.3
HY_SLOW_PCT = 1.5
NA_HEADS = 8
NA_WIN_ROWS = 8
NA_WIN_COLS = 16
ML_HEADS = 8
ML_TILE = 256
ML_M0 = -1e30
ROPE_BASE = 10000.0
PEER_HEADS = 8
PEER_NKEYS = 128
PEER_TOPK = 16
LN_EPS = 1e-6

_VMEM_LIMIT = 52 * 1024 * 1024
_NEG = -1e30

_NT = (((1,), (1,)), ((), ()))
_TN = (((0,), (0,)), ((), ()))


def _cp(*sem):
    return pltpu.CompilerParams(dimension_semantics=sem, vmem_limit_bytes=_VMEM_LIMIT)


def _tile(n, cap, align):
    t = (min(cap, n) // align) * align
    while t >= align:
        if n % t == 0:
            return t
        t -= align
    return n


def _split(x):
    hi = x.astype(_BF16)
    lo = (x.astype(_F32) - hi.astype(_F32)).astype(_BF16)
    return hi, lo


def _dot(a, b, passes=1, dims=None):
    dn = dims or (((a.ndim - 1,), (0,)), ((), ()))
    f = lambda x, y: lax.dot_general(x, y, dn, preferred_element_type=_F32)
    if passes == 1:
        return f(a.astype(_BF16), b.astype(_BF16))
    ah, al = _split(a)
    bh, bl = _split(b)
    return f(ah, bh) + (f(ah, bl) + f(al, bh))


def _sigmoid(x):
    return 1.0 / (1.0 + jnp.exp(-x))


def _mm_kernel(*refs, pre, has_bias, passes):
    it = iter(refs)
    a_ref = next(it)
    if pre == "mod":
        sc_ref, sh_ref = next(it), next(it)
    b_ref = next(it)
    bias_ref = next(it) if has_bias else None
    o_ref = next(it)
    a = a_ref[...]
    if pre == "mod":
        a = a.astype(_F32) * (1.0 + sc_ref[...]) + sh_ref[...]
    elif pre == "silu":
        a = a * _sigmoid(a)
    acc = _dot(a, b_ref[...], passes)
    if has_bias:
        acc = acc + bias_ref[...]
    o_ref[...] = acc.astype(o_ref.dtype)


def _mm(a, b, *, scale=None, shift=None, pre=None, bias=None, out_dtype=_F32, passes=1, name="mm"):
    m, k = a.shape
    n = b.shape[1]
    if scale is not None:
        pre = "mod"
    tile_bytes = 4 * 1024 * 1024
    tm = _tile(m, max(8, min(1024, 2 * tile_bytes // (k * a.dtype.itemsize))), 8)
    tn = _tile(n, max(128, min(8192, tile_bytes // (k * b.dtype.itemsize), tile_bytes // (tm * 4))), 128)
    in_specs = [pl.BlockSpec((tm, k), lambda i, j: (i, 0))]
    args = [a]
    if pre == "mod":
        in_specs += [pl.BlockSpec((1, k), lambda i, j: (0, 0))] * 2
        args += [scale, shift]
    in_specs.append(pl.BlockSpec((k, tn), lambda i, j: (0, j)))
    args.append(b)
    if bias is not None:
        in_specs.append(pl.BlockSpec((1, tn), lambda i, j: (0, j)))
        args.append(bias)
    return pl.pallas_call(
        functools.partial(_mm_kernel, pre=pre, has_bias=bias is not None, passes=passes),
        grid=(m // tm, n // tn),
        in_specs=in_specs,
        out_specs=pl.BlockSpec((tm, tn), lambda i, j: (i, j)),
        out_shape=jax.ShapeDtypeStruct((m, n), out_dtype),
        compiler_params=_cp("parallel", "parallel"),
        name=name,
    )(*args)


def _ln_res_kernel(h_ref, y_ref, g_ref, w_ref, b_ref, sc_ref, sh_ref, *o_refs, alpha, with_mod):
    z = alpha * h_ref[...] + g_ref[...] * y_ref[...]
    mu = jnp.mean(z, axis=-1, keepdims=True)
    zc = z - mu
    var = jnp.mean(zc * zc, axis=-1, keepdims=True)
    hn = zc * lax.rsqrt(var + LN_EPS) * w_ref[...] + b_ref[...]
    o_refs[0][...] = hn
    if with_mod:
        o_refs[1][...] = hn * (1.0 + sc_ref[...]) + sh_ref[...]


def _ln_res(h, y, g, w, b, sc, sh, *, alpha, with_mod):
    m, d = h.shape
    tr = _tile(m, 256, 8)
    row = pl.BlockSpec((tr, d), lambda i: (i, 0))
    vec = pl.BlockSpec((1, d), lambda i: (0, 0))
    n_out = 2 if with_mod else 1
    out = pl.pallas_call(
        functools.partial(_ln_res_kernel, alpha=alpha, with_mod=with_mod),
        grid=(m // tr,),
        in_specs=[row, row, vec, vec, vec, vec, vec],
        out_specs=[row] * n_out,
        out_shape=[jax.ShapeDtypeStruct((m, d), _F32)] * n_out,
        compiler_params=_cp("parallel"),
        name="ln_res",
    )(h, y, g, w, b, sc, sh)
    return out if with_mod else (out[0], None)


def _conv3_kernel(*refs, n_out, act, rope, out_scales):
    it = iter(refs)
    xp_ref, x_ref, xn_ref, w_ref, b_ref = (next(it) for _ in range(5))
    if rope:
        cos_ref, sin_ref = next(it), next(it)
    o_refs = [next(it) for _ in range(n_out)]
    i = pl.program_id(0)
    x = x_ref[...]
    tr = x.shape[0]
    rows = lax.broadcasted_iota(jnp.int32, x.shape, 0)
    prev_row = jnp.where(i > 0, xp_ref[7:8, :], 0.0)
    next_row = jnp.where(i < pl.num_programs(0) - 1, xn_ref[0:1, :], 0.0)
    x_m1 = jnp.where(rows == 0, prev_row, pltpu.roll(x, 1, axis=0))
    x_p1 = jnp.where(rows == tr - 1, next_row, pltpu.roll(x, tr - 1, axis=0))
    y = b_ref[...] + x_m1 * w_ref[0:1, :] + x * w_ref[1:2, :] + x_p1 * w_ref[2:3, :]
    if act:
        y = y * _sigmoid(y)
    wo = y.shape[1] // n_out
    for o in range(n_out):
        yo = y[:, o * wo:(o + 1) * wo]
        if rope:
            cos, sin = cos_ref[...], sin_ref[...]
            lane = lax.broadcasted_iota(jnp.int32, cos.shape, 1)
            first = (lane % 64) < 32
            parts = []
            for hh in range(wo // 128):
                yh = yo[:, hh * 128:(hh + 1) * 128]
                sw = jnp.where(first, pltpu.roll(yh, 96, axis=1), pltpu.roll(yh, 32, axis=1))
                parts.append(yh * cos + sw * sin)
            yo = jnp.concatenate(parts, axis=1)
        if out_scales[o] != 1.0:
            yo = yo * out_scales[o]
        o_refs[o][...] = yo.astype(o_refs[o].dtype)


def _conv3(x, w, b, *, n_out, out_dtype, act=False, rope=None, out_scales=None):
    l, cx = x.shape
    tr = _tile(l, 256, 8)
    nb8 = l // 8
    out_scales = out_scales or (1.0,) * n_out
    in_specs = [
        pl.BlockSpec((8, cx), lambda i: (jnp.maximum(i * (tr // 8) - 1, 0), 0)),
        pl.BlockSpec((tr, cx), lambda i: (i, 0)),
        pl.BlockSpec((8, cx), lambda i: (jnp.minimum((i + 1) * (tr // 8), nb8 - 1), 0)),
        pl.BlockSpec((3, cx), lambda i: (0, 0)),
        pl.BlockSpec((1, cx), lambda i: (0, 0)),
    ]
    args = [x, x, x, w, b.reshape(1, cx)]
    if rope is not None:
        in_specs += [pl.BlockSpec((tr, 128), lambda i: (i, 0))] * 2
        args += list(rope)
    wo = cx // n_out
    return pl.pallas_call(
        functools.partial(_conv3_kernel, n_out=n_out, act=act, rope=rope is not None, out_scales=out_scales),
        grid=(l // tr,),
        in_specs=in_specs,
        out_specs=[pl.BlockSpec((tr, wo), lambda i: (i, 0))] * n_out,
        out_shape=[jax.ShapeDtypeStruct((l, wo), out_dtype)] * n_out,
        compiler_params=_cp("parallel"),
        name="conv3",
    )(*args)


def _hy_filter_kernel(f_ref, t_ref, w1_ref, b1_ref, w2_ref, b2_ref, w3_ref, b3_ref, w4_ref, d_ref, o_ref):
    h = jnp.sin(_dot(f_ref[...], w1_ref[...], 3) + b1_ref[...])
    h = jnp.sin(_dot(h, w2_ref[...], 3) + b2_ref[...])
    h = jnp.sin(_dot(h, w3_ref[...], 3) + b3_ref[...])
    taps = _dot(h, w4_ref[...], 3)
    t = t_ref[...]
    window = jnp.exp(-t * d_ref[...])
    c = window.shape[1]
    tl = t.shape[0]
    row = pl.program_id(0) * tl + lax.broadcasted_iota(jnp.int32, (tl, 1), 0)
    not_first = jnp.where(row == 0, 0.0, 1.0)
    n_grp = taps.shape[1] // c
    for g in range(n_grp):
        blk = taps[:, g * c:(g + 1) * c] * window
        if g >= n_grp // 2:
            blk = blk * not_first
        o_ref[:, g * c:(g + 1) * c] = blk.astype(o_ref.dtype)


def _hy_filter(seq_len, fw1, fb1, fw2, fb2, fw3, fb3, fw4, width):
    hid = fw1.shape[1]
    t = jnp.linspace(0.0, 1.0, seq_len, dtype=_F32)[:, None]
    omega = 2.0 * math.pi * jnp.arange(seq_len, dtype=_F32) / seq_len
    bands = (HY_EMB - 1) // 2
    freqs = jnp.linspace(1e-4, bands - 1, bands, dtype=_F32)
    ang = omega[:, None] * freqs[None, :]
    kpad = 64
    feats = jnp.concatenate([t, jnp.cos(ang), -jnp.sin(ang), jnp.zeros((seq_len, kpad - HY_EMB), _F32)], axis=-1)
    w1 = jnp.concatenate([fw1, jnp.zeros((kpad - HY_EMB, hid), _F32)], axis=0)
    max_decay = math.log(HY_DECAY_TARGET) / HY_FAST_PCT
    min_decay = math.log(HY_DECAY_TARGET) / HY_SLOW_PCT
    deltas = jnp.abs(jnp.linspace(min_decay, max_decay, width, dtype=_F32))[None, :]
    n_cols = fw4.shape[1]
    tl = _tile(seq_len, 256, 8)
    full = lambda shape: pl.BlockSpec(shape, lambda i: (0,) * len(shape))
    return pl.pallas_call(
        _hy_filter_kernel,
        grid=(seq_len // tl,),
        in_specs=[pl.BlockSpec((tl, kpad), lambda i: (i, 0)), pl.BlockSpec((tl, 1), lambda i: (i, 0)),
                  full((kpad, hid)), full((1, hid)), full((hid, hid)), full((1, hid)), full((hid, hid)), full((1, hid)),
                  full((hid, n_cols)), full((1, width))],
        out_specs=pl.BlockSpec((tl, n_cols), lambda i: (i, 0)),
        out_shape=jax.ShapeDtypeStruct((seq_len, n_cols), _BF16),
        compiler_params=_cp("parallel"),
        name="hy_filter",
    )(feats, t, w1, fb1.reshape(1, hid), fw2, fb2.reshape(1, hid), fw3, fb3.reshape(1, hid), fw4, deltas)


def _fft_consts(n1, n2):
    n = n1 * n2
    k1 = np.arange(n1)[:, None]
    j1 = np.arange(n1 // 2)[None, :]
    ang1 = 2.0 * np.pi * k1 * j1 / n1
    f1_fwd = np.concatenate([np.cos(ang1), -np.sin(ang1)], axis=0)
    f1_inv = np.concatenate([np.cos(ang1).T, -np.sin(ang1).T], axis=1) / n
    k2 = np.arange(n2)[:, None]
    j2 = np.arange(n2)[None, :]
    ang2 = 2.0 * np.pi * k2 * j2 / n2
    f2 = np.stack([np.cos(ang2), -np.sin(ang2)])
    angt = 2.0 * np.pi * np.arange(n1)[:, None] * np.arange(n2)[None, :] / n
    tw = np.stack([np.cos(angt), -np.sin(angt)], axis=1)
    as32 = lambda a: jnp.asarray(a, dtype=_F32)
    return as32(f1_fwd), as32(f1_inv), as32(f2), as32(tw), as32(np.transpose(tw, (0, 2, 1)))


def _cmm(f_stack, xr, xi):
    n = f_stack.shape[0] // 2
    p1 = jnp.dot(f_stack, xr, preferred_element_type=_F32)
    p2 = jnp.dot(f_stack, xi, preferred_element_type=_F32)
    return p1[:n] - p2[n:], p2[:n] + p1[n:]


def _twiddled_dft(f2_ref, tw_ref):
    f2r, f2i = f2_ref[0], f2_ref[1]
    tr, ti = tw_ref[0, 0:1, :], tw_ref[0, 1:2, :]
    return jnp.concatenate([f2r * tr - f2i * ti, f2r * ti + f2i * tr], axis=0).astype(_BF16)


def _fft_spec_kernel(af_ref, ab_ref, f2_ref, tw_ref, o_ref):
    fw = _twiddled_dft(f2_ref, tw_ref)
    fr, fi = _cmm(fw, af_ref[0, 0], af_ref[1, 0])
    br, bi = _cmm(fw, ab_ref[0, 0], ab_ref[1, 0])
    o_ref[0, 0, 0] = fr + br
    o_ref[0, 1, 0] = fi - bi


def _fft_mid_kernel(a_ref, h_ref, f2_ref, tw_ref, twc_ref, o_ref):
    fw = _twiddled_dft(f2_ref, tw_ref)
    br, bi = _cmm(fw, a_ref[0, 0], a_ref[1, 0])
    hr, hi = h_ref[0, 0, 0], h_ref[0, 1, 0]
    yr = (br * hr - bi * hi).astype(_BF16)
    yi = (br * hi + bi * hr).astype(_BF16)
    f2r, f2i = f2_ref[0], f2_ref[1]
    tcr, tci = twc_ref[0, :, 0:1], twc_ref[0, :, 1:2]
    g = jnp.concatenate([tcr * f2r - tci * f2i, -(tcr * f2i + tci * f2r)], axis=0).astype(_BF16)
    zr, zi = _cmm(g, yr, yi)
    o_ref[0, 0] = zr.astype(o_ref.dtype)
    o_ref[1, 0] = zi.astype(o_ref.dtype)


def _fft_shape(l_pad):
    n2 = 256 if l_pad >= 8192 else 64
    n1 = 2 * l_pad // n2
    return n1, n2


def _hy_spectra(filt, width, consts, n1, n2):
    f1_fwd, _, f2, tw, _ = consts
    n_cols = filt.shape[1]
    a = _mm(f1_fwd, filt.reshape(n1 // 2, n2 * n_cols), out_dtype=_BF16, name="fft_a_filt")
    a = a.reshape(2, n1, n2, n_cols)
    blk = lambda d: pl.BlockSpec((2, 1, n2, width), lambda k, o: (0, k, 0, d * HY_ORDER + o))
    return pl.pallas_call(
        _fft_spec_kernel,
        grid=(n1, HY_ORDER),
        in_specs=[blk(0), blk(1), pl.BlockSpec((2, n2, n2), lambda k, o: (0, 0, 0)),
                  pl.BlockSpec((1, 2, n2), lambda k, o: (k, 0, 0))],
        out_specs=pl.BlockSpec((1, 2, 1, n2, width), lambda k, o: (o, 0, k, 0, 0)),
        out_shape=jax.ShapeDtypeStruct((HY_ORDER, 2, n1, n2, width), _F32),
        compiler_params=_cp("parallel", "parallel"),
        name="fft_spec",
    )(a, a, f2, tw)


def _long_conv(z, spec, order, consts, n1, n2):
    f1_fwd, f1_inv, f2, tw, twc = consts
    l_pad, c = z.shape
    a = _mm(f1_fwd, z.reshape(n1 // 2, n2 * c), out_dtype=_BF16, name="fft_a")
    a = a.reshape(2, n1, n2, c)
    blk = pl.BlockSpec((2, 1, n2, c), lambda k: (0, k, 0, 0))
    mid = pl.pallas_call(
        _fft_mid_kernel,
        grid=(n1,),
        in_specs=[blk, pl.BlockSpec((1, 2, 1, n2, c), lambda k: (order, 0, k, 0, 0)),
                  pl.BlockSpec((2, n2, n2), lambda k: (0, 0, 0)),
                  pl.BlockSpec((1, 2, n2), lambda k: (k, 0, 0)), pl.BlockSpec((1, n2, 2), lambda k: (k, 0, 0))],
        out_specs=blk,
        out_shape=jax.ShapeDtypeStruct((2, n1, n2, c), _BF16),
        compiler_params=_cp("parallel"),
        name="fft_mid",
    )(a, spec, f2, tw, twc)
    y = _mm(f1_inv, mid.reshape(2 * n1, n2 * c), name="fft_c")
    return y.reshape(l_pad, c)


def _hy_gate_kernel(y_ref, z_ref, s_ref, g_ref, o_ref):
    o_ref[...] = (g_ref[...] * (y_ref[...] + z_ref[...] * s_ref[...])).astype(o_ref.dtype)


def _hy_gate(y, z, skip, gate, out_dtype):
    l, c = y.shape
    tr = _tile(l, 512, 8)
    row = pl.BlockSpec((tr, c), lambda i: (i, 0))
    return pl.pallas_call(
        _hy_gate_kernel,
        grid=(l // tr,),
        in_specs=[row, row, pl.BlockSpec((1, c), lambda i: (0, 0)), row],
        out_specs=row,
        out_shape=jax.ShapeDtypeStruct((l, c), out_dtype),
        compiler_params=_cp("parallel"),
        name="hy_gate",
    )(y, z, skip.reshape(1, c), gate)


def _hyena(p, conv_w, conv_b, fparams, skip):
    l = p.shape[0]
    width = p.shape[1] // (HY_ORDER + 1)
    l_pad = max(l, 1024)
    n1, n2 = _fft_shape(l_pad)
    consts = _fft_consts(n1, n2)
    filt = _hy_filter(l, *fparams, width)
    parts = _conv3(p, conv_w, conv_b, n_out=HY_ORDER + 1, out_dtype=_F32)
    if l_pad != l:
        pad = lambda a: jnp.pad(a, ((0, l_pad - l), (0, 0)))
        filt = pad(filt)
        parts = [pad(a) for a in parts]
    spec = _hy_spectra(filt, width, consts, n1, n2)
    z = parts[0]
    for o in range(HY_ORDER):
        y = _long_conv(z, spec, o, consts, n1, n2)
        z = _hy_gate(y, z, skip[o], parts[o + 1], _F32 if o + 1 < HY_ORDER else _BF16)
    return z[:l]


def _na_bias_table(rpb, w):
    kr, kc = NA_WIN_ROWS, NA_WIN_COLS
    cols = np.arange(w)
    start = np.clip(cols - kc // 2, 0, w - kc)
    kcol = np.arange(w)[None, :]
    in_win = (kcol >= start[:, None]) & (kcol < start[:, None] + kc)
    rp = jnp.pad(rpb, ((0, 0), (0, 0), (w - 1, w)))
    by_q = jnp.stack([rp[:, :, kc - 1 + w - 1 - q:kc - 1 + 2 * w - 1 - q] for q in range(w)], axis=2)
    by_q = jnp.where(in_win[None, None], by_q, _NEG)
    t = jnp.stack([jnp.concatenate([by_q[:, i - d + kr - 1] for i in range(kr)], axis=-1) for d in range(kr)], axis=1)
    return t


def _natten_kernel(q_ref, k_ref, v_ref, kc_ref, vc_ref, bias_ref, o_ref, *, rows, w, scale):
    qb = pl.program_id(1)
    kr = NA_WIN_ROWS
    kc, vc = kc_ref[...], vc_ref[...]
    for a in range(kr):
        r = qb * kr + a
        r0 = jnp.clip(r - kr // 2, 0, rows - kr)
        d = r - r0
        start = pl.multiple_of(r0 * w, w)
        kwin = k_ref[pl.ds(start, kr * w), :]
        vwin = v_ref[pl.ds(start, kr * w), :]
        qa = q_ref[a * w:(a + 1) * w, :]
        s_lat = _dot(qa, kwin, dims=_NT) * scale + bias_ref[0, d]
        s_ctx = _dot(qa, kc, dims=_NT) * scale
        m = jnp.maximum(jnp.max(s_lat, axis=-1, keepdims=True), jnp.max(s_ctx, axis=-1, keepdims=True))
        p_lat = jnp.exp(s_lat - m)
        p_ctx = jnp.exp(s_ctx - m)
        den = jnp.sum(p_lat, axis=-1, keepdims=True) + jnp.sum(p_ctx, axis=-1, keepdims=True)
        o = _dot(p_lat, vwin) + _dot(p_ctx, vc)
        o_ref[a * w:(a + 1) * w, :] = (o / den).astype(o_ref.dtype)


def _natten(qkv, kv_ctx, rpb):
    l = qkv.shape[0]
    h = NA_HEADS
    hd = qkv.shape[1] // (3 * h)
    w = GRID_W
    rows = l // w
    kr = NA_WIN_ROWS
    assert rows % kr == 0 and rows >= kr and hd % 128 == 0
    lc = kv_ctx.shape[0]
    bias = _na_bias_table(rpb, w)
    tq = kr * w
    return pl.pallas_call(
        functools.partial(_natten_kernel, rows=rows, w=w, scale=hd ** -0.5),
        grid=(h, rows // kr),
        in_specs=[pl.BlockSpec((tq, hd), lambda hh, qb: (qb, hh)),
                  pl.BlockSpec((l, hd), lambda hh, qb: (0, h + hh)),
                  pl.BlockSpec((l, hd), lambda hh, qb: (0, 2 * h + hh)),
                  pl.BlockSpec((lc, hd), lambda hh, qb: (0, h + hh)),
                  pl.BlockSpec((lc, hd), lambda hh, qb: (0, 2 * h + hh)),
                  pl.BlockSpec((1, kr, w, kr * w), lambda hh, qb: (hh, 0, 0, 0))],
        out_specs=pl.BlockSpec((tq, hd), lambda hh, qb: (qb, hh)),
        out_shape=jax.ShapeDtypeStruct((l, h * hd), _BF16),
        compiler_params=_cp("parallel", "parallel"),
        name="natten",
    )(qkv, qkv, qkv, kv_ctx, kv_ctx, bias)


def _ctx_attn_kernel(q_ref, k_ref, v_ref, o_ref, *, scale):
    s = _dot(q_ref[...], k_ref[...], dims=_NT) * scale
    p = jnp.exp(s - jnp.max(s, axis=-1, keepdims=True))
    den = jnp.sum(p, axis=-1, keepdims=True)
    o_ref[...] = (_dot(p, v_ref[...]) / den).astype(o_ref.dtype)


def _ctx_attn(qkv):
    lc = qkv.shape[0]
    h = NA_HEADS
    hd = qkv.shape[1] // (3 * h)
    blk = lambda off: pl.BlockSpec((lc, hd), lambda hh: (0, off + hh))
    return pl.pallas_call(
        functools.partial(_ctx_attn_kernel, scale=hd ** -0.5),
        grid=(h,),
        in_specs=[blk(0), blk(h), blk(2 * h)],
        out_specs=blk(0),
        out_shape=jax.ShapeDtypeStruct((lc, h * hd), _BF16),
        compiler_params=_cp("parallel"),
        name="ctx_attn",
    )(qkv, qkv, qkv)


def _ml_gates_kernel(g_ref, b_ref, o_ref, *, heads):
    g = g_ref[...] + b_ref[...]
    logsig = jnp.minimum(g, 0.0) - jnp.log(1.0 + jnp.exp(-jnp.abs(g)))
    lane = lax.broadcasted_iota(jnp.int32, g.shape, 1)
    is_forget = (lane // heads) % 2 == 1
    o_ref[...] = jnp.where(is_forget, logsig, g)


def _ml_gates(g_raw, b_gates):
    l, n = g_raw.shape
    tr = _tile(l, 1024, 8)
    bias = jnp.pad(b_gates, (0, n - b_gates.shape[0])).reshape(1, n)
    return pl.pallas_call(
        functools.partial(_ml_gates_kernel, heads=ML_HEADS),
        grid=(l // tr,),
        in_specs=[pl.BlockSpec((tr, n), lambda i: (i, 0)), pl.BlockSpec((1, n), lambda i: (0, 0))],
        out_specs=pl.BlockSpec((tr, n), lambda i: (i, 0)),
        out_shape=jax.ShapeDtypeStruct((l, n), _F32),
        compiler_params=_cp("parallel"),
        name="ml_gates",
    )(g_raw, bias)


def _ml_scan_kernel(*refs, heads, dk, dv, reverse, with_h):
    it = iter(refs)
    if with_h:
        q_ref = next(it)
    k_ref, v_ref, g_ref, gt_ref, c0_ref, n0_ref, m0_ref = (next(it) for _ in range(7))
    if with_h:
        h_ref = next(it)
    c_ref, n_ref, m_ref = next(it), next(it), next(it)

    @pl.when(pl.program_id(0) == 0)
    def _():
        c_ref[...] = c0_ref[...]
        n_ref[...] = n0_ref[...]
        m_ref[...] = m0_ref[...]

    t = k_ref.shape[0]
    ti = lax.broadcasted_iota(jnp.int32, (t, t), 0)
    si = lax.broadcasted_iota(jnp.int32, (t, t), 1)
    seen = (si >= ti) if reverse else (si <= ti)
    seen_t = (ti >= si) if reverse else (ti <= si)
    g = g_ref[...]
    gt = gt_ref[0]
    c0 = 2 * heads if reverse else 0
    for h in range(heads):
        i_col = g[:, c0 + h:c0 + h + 1]
        f_col = g[:, c0 + heads + h:c0 + heads + h + 1]
        i_row = gt[c0 + h:c0 + h + 1, :]
        f_row = gt[c0 + heads + h:c0 + heads + h + 1, :]
        b_col = jnp.sum(jnp.where(seen, f_row, 0.0), axis=1, keepdims=True)
        b_row = jnp.sum(jnp.where(seen_t, f_col, 0.0), axis=0, keepdims=True)
        b_end = jnp.sum(f_row, axis=1, keepdims=True)
        m_prev = m_ref[h][:, 0:1]
        w_log_row = b_end - b_row + i_row
        w_log_col = b_end - b_col + i_col
        m_new = jnp.maximum(b_end + m_prev, jnp.max(w_log_row, axis=1, keepdims=True))
        decay = jnp.exp(b_end + m_prev - m_new)
        w_col = jnp.exp(w_log_col - m_new)
        kh = k_ref[:, h * dk:(h + 1) * dk]
        vh = v_ref[:, h * dv:(h + 1) * dv]
        ct = c_ref[h]
        nv = n_ref[h]
        if with_h:
            qh = q_ref[:, h * dk:(h + 1) * dk]
            log_d = jnp.where(seen, b_col - b_row + i_row, -jnp.inf)
            m_in = b_col + m_prev
            m_t = jnp.maximum(m_in, jnp.max(log_d, axis=1, keepdims=True))
            sc = _dot(qh, kh, dims=_NT) * jnp.exp(log_d - m_t)
            g_in = jnp.exp(m_in - m_t)
            num = _dot(sc, vh) + _dot(qh, ct) * g_in
            den = jnp.sum(sc, axis=1, keepdims=True) + jnp.sum(qh.astype(_F32) * nv, axis=1, keepdims=True) * g_in
            h_ref[:, h * dv:(h + 1) * dv] = num / jnp.maximum(jnp.abs(den), jnp.exp(-m_t))
        vw = vh.astype(_F32) * w_col
        c_ref[h] = decay * ct + _dot(kh, vw, dims=_TN)
        n_ref[h] = decay * nv + jnp.sum(kh.astype(_F32) * w_col, axis=0, keepdims=True)
        m_ref[h] = jnp.broadcast_to(m_new, m_ref.shape[1:])


def _ml_scan(q, k, v, gates, state, *, reverse, dk, dv):
    l = k.shape[0]
    heads = ML_HEADS
    t = min(ML_TILE, l)
    assert l % t == 0
    nc = l // t
    with_h = q is not None
    gt = jnp.transpose(gates.reshape(nc, t, gates.shape[1]), (0, 2, 1))
    cidx = (lambda j: nc - 1 - j) if reverse else (lambda j: j)
    row = lambda width: pl.BlockSpec((t, width), lambda j: (cidx(j), 0))
    st_specs = [pl.BlockSpec((heads, dk, dv), lambda j: (0, 0, 0)), pl.BlockSpec((heads, 1, dk), lambda j: (0, 0, 0)),
                pl.BlockSpec((heads, 1, 128), lambda j: (0, 0, 0))]
    st_shapes = [jax.ShapeDtypeStruct((heads, dk, dv), _F32), jax.ShapeDtypeStruct((heads, 1, dk), _F32),
                 jax.ShapeDtypeStruct((heads, 1, 128), _F32)]
    in_specs = ([row(heads * dk)] if with_h else []) + [
        row(heads * dk), row(heads * dv), row(gates.shape[1]),
        pl.BlockSpec((1, gates.shape[1], t), lambda j: (cidx(j), 0, 0))] + st_specs
    args = ([q] if with_h else []) + [k, v, gates, gt, *state]
    out_specs = ([row(heads * dv)] if with_h else []) + st_specs
    out_shape = ([jax.ShapeDtypeStruct((l, heads * dv), _F32)] if with_h else []) + st_shapes
    out = pl.pallas_call(
        functools.partial(_ml_scan_kernel, heads=heads, dk=dk, dv=dv, reverse=reverse, with_h=with_h),
        grid=(nc,),
        in_specs=in_specs,
        out_specs=out_specs,
        out_shape=out_shape,
        compiler_params=_cp("arbitrary"),
        name="ml_scan",
    )(*args)
    return (out[0], tuple(out[1:])) if with_h else (None, tuple(out))


def _ml_merge_kernel(hf_ref, hb_ref, o_ref, w_ref, out_ref, *, heads, dv):
    for h in range(heads):
        sl = slice(h * dv, (h + 1) * dv)
        x = hf_ref[:, sl] + hb_ref[:, sl]
        mu = jnp.mean(x, axis=-1, keepdims=True)
        xc = x - mu
        var = jnp.mean(xc * xc, axis=-1, keepdims=True)
        hn = xc * lax.rsqrt(var + LN_EPS) * w_ref[:, sl]
        out_ref[:, sl] = (hn * _sigmoid(o_ref[:, sl])).astype(out_ref.dtype)


def _ml_merge(hf, hb, o, norm_w, dv):
    l, n = hf.shape
    tr = _tile(l, 256, 8)
    row = pl.BlockSpec((tr, n), lambda i: (i, 0))
    return pl.pallas_call(
        functools.partial(_ml_merge_kernel, heads=ML_HEADS, dv=dv),
        grid=(l // tr,),
        in_specs=[row, row, row, pl.BlockSpec((1, n), lambda i: (0, 0))],
        out_specs=row,
        out_shape=jax.ShapeDtypeStruct((l, n), _BF16),
        compiler_params=_cp("parallel"),
        name="ml_merge",
    )(hf, hb, o, norm_w.reshape(1, n))


def _rope_tables(l):
    quarter = 32
    pos = jnp.arange(l)
    inv = ROPE_BASE ** (-jnp.arange(quarter, dtype=_F32) / quarter)
    ang_r = (pos // GRID_W).astype(_F32)[:, None] * inv[None]
    ang_c = (pos % GRID_W).astype(_F32)[:, None] * inv[None]
    cos = jnp.concatenate([jnp.cos(ang_r)] * 2 + [jnp.cos(ang_c)] * 2, axis=-1)
    sin = jnp.concatenate([-jnp.sin(ang_r), jnp.sin(ang_r), -jnp.sin(ang_c), jnp.sin(ang_c)], axis=-1)
    return cos, sin


def _mlstm_mixer(h_lat, h_ctx, mod_l, mod_c, w_in, b_gates, conv_w, conv_b, norm_w, w_out):
    heads = ML_HEADS
    v_w = w_out.shape[0]
    dv = v_w // heads
    dk = dv // 2
    qk_w = heads * dk
    g0 = 2 * qk_w + 2 * v_w
    n_g = w_in.shape[1] - g0
    assert dk == 128
    l = h_lat.shape[0]
    wb = w_in.astype(_BF16)
    w_g = jnp.pad(w_in[:, g0:], ((0, 0), (0, 128 - n_g)))
    sc_l, sh_l = mod_l
    sc_c, sh_c = mod_c
    qk_l = _mm(h_lat, wb[:, :2 * qk_w], scale=sc_l, shift=sh_l, name="ml_proj_qk")
    v_l = _mm(h_lat, wb[:, 2 * qk_w:2 * qk_w + v_w], scale=sc_l, shift=sh_l, out_dtype=_BF16, name="ml_proj_v")
    o_l = _mm(h_lat, wb[:, 2 * qk_w + v_w:g0], scale=sc_l, shift=sh_l, name="ml_proj_o")
    g_l = _ml_gates(_mm(h_lat, w_g, scale=sc_l, shift=sh_l, passes=3, name="ml_proj_g"), b_gates)
    k_c_raw = _mm(h_ctx, wb[:, qk_w:2 * qk_w], scale=sc_c, shift=sh_c, name="ml_proj_kc")
    v_c = _mm(h_ctx, wb[:, 2 * qk_w:2 * qk_w + v_w], scale=sc_c, shift=sh_c, out_dtype=_BF16, name="ml_proj_vc")
    g_c = _ml_gates(_mm(h_ctx, w_g, scale=sc_c, shift=sh_c, passes=3, name="ml_proj_gc"), b_gates)
    k_scale = dk ** -0.5
    q_l, k_l = _conv3(qk_l, conv_w, conv_b, n_out=2, out_dtype=_BF16, act=True, rope=_rope_tables(l),
                      out_scales=(1.0, k_scale))
    (k_c,) = _conv3(k_c_raw, conv_w[:, qk_w:], conv_b[qk_w:], n_out=1, out_dtype=_BF16, act=True,
                    out_scales=(k_scale,))
    init = (jnp.zeros((heads, dk, dv), _F32), jnp.zeros((heads, 1, dk), _F32), jnp.full((heads, 1, 128), ML_M0, _F32))
    scan = functools.partial(_ml_scan, dk=dk, dv=dv)
    _, st_f = scan(None, k_c, v_c, g_c, init, reverse=False)
    _, st_b = scan(None, k_c, v_c, g_c, init, reverse=True)
    hl_f, _ = scan(q_l, k_l, v_l, g_l, st_f, reverse=False)
    hl_b, _ = scan(q_l, k_l, v_l, g_l, st_b, reverse=True)
    merged = _ml_merge(hl_f, hl_b, o_l, norm_w, dv)
    return _mm(merged, w_out.astype(_BF16), name="ml_out")


def _top_desc(x, count):
    r = x.shape[0]
    idx = lax.broadcasted_iota(jnp.int32, x.shape, 0).astype(_F32)
    vals = []
    for _ in range(count):
        m = jnp.max(x, axis=0, keepdims=True)
        vals.append(m)
        first = jnp.min(jnp.where(x == m, idx, float(r)), axis=0, keepdims=True)
        x = jnp.where(idx == first, -jnp.inf, x)
    return vals


def _peer_route_kernel(q_ref, k1_ref, k2_ref, e1_ref, e2_ref, th_ref):
    half = k1_ref.shape[1]
    q = q_ref[...]
    s1 = _dot(k1_ref[...], q[:, :half], 3, dims=_NT)
    s2 = _dot(k2_ref[...], q[:, half:], 3, dims=_NT)
    n_top = PEER_TOPK + 1
    v1 = _top_desc(s1, n_top)
    v2 = _top_desc(s2, n_top)
    cands = [v1[a] + v2[b] for a in range(n_top) for b in range(n_top) if (a + 1) * (b + 1) <= n_top]
    pad = (-len(cands)) % 8
    cand = jnp.concatenate(cands + [jnp.full_like(cands[0], -jnp.inf)] * pad, axis=0)
    top = _top_desc(cand, n_top)
    m = top[0]
    z = top[0] * 0.0
    for j in range(PEER_TOPK):
        z = z + jnp.exp(top[j] - m)
    tau = 0.5 * (top[PEER_TOPK - 1] + top[PEER_TOPK])
    inv_z = 1.0 / z
    e1_ref[0] = jnp.exp(s1 - v1[0])
    e2_ref[0] = jnp.exp(s2 - v2[0]) * inv_z
    th_ref[0] = jnp.exp(tau - m) * inv_z


def _peer_dense_kernel(x_ref, u_ref, vt_ref, e1_ref, e2_ref, th_ref, o_ref, act_ref, a_ref, *, heads, nkeys):
    @pl.when(pl.program_id(1) == 0)
    def _():
        o_ref[...] = jnp.zeros_like(o_ref)

    n_sub = u_ref.shape[0] // nkeys
    n_part = 2
    part = u_ref.shape[0] // n_part
    lanes = 128

    def activations(pt):
        rows = slice(pt * part, (pt + 1) * part)
        act_ref[rows, :] = _dot(u_ref[rows, :], x_ref[...], dims=_NT)

    def gated(i):
        rows = slice(i * nkeys, (i + 1) * nkeys)
        for c in range(x_ref.shape[0] // lanes):
            cols = slice(c * lanes, (c + 1) * lanes)
            gate = None
            for h in range(heads):
                p = e1_ref[h, i:i + 1, cols] * e2_ref[h, :, cols]
                sel = jnp.where(p >= th_ref[h, :, cols], p, 0.0)
                gate = sel if gate is None else gate + sel
            act = act_ref[rows, cols]
            gelu = 0.5 * act * (1.0 + lax.erf(act * (2.0 ** -0.5)))
            a_ref[rows, cols] = (gate * gelu).astype(a_ref.dtype)

    activations(0)
    for pt in range(n_part):
        if pt + 1 < n_part:
            activations(pt + 1)
        for i in range(pt * n_sub // n_part, (pt + 1) * n_sub // n_part):
            gated(i)
        rows = slice(pt * part, (pt + 1) * part)
        o_ref[...] += jnp.dot(vt_ref[:, rows], a_ref[rows, :], preferred_element_type=_F32)


def _peer(x, wq, k1, k2, u_bf, vt_bf):
    t, d = x.shape
    heads, nkeys = PEER_HEADS, PEER_NKEYS
    qdim = wq.shape[1] // heads
    q = _mm(x, wq, passes=3, name="peer_q")
    tt = _tile(t, 512, 128)
    e_shape = jax.ShapeDtypeStruct((heads, nkeys, t), _F32)
    e_blk = pl.BlockSpec((1, nkeys, tt), lambda i, h: (h, 0, i))
    th_blk = pl.BlockSpec((1, 1, tt), lambda i, h: (h, 0, i))
    e1, e2, th = pl.pallas_call(
        _peer_route_kernel,
        grid=(t // tt, heads),
        in_specs=[pl.BlockSpec((tt, qdim), lambda i, h: (i, h)),
                  pl.BlockSpec((nkeys, qdim // 2), lambda i, h: (0, 0)),
                  pl.BlockSpec((nkeys, qdim // 2), lambda i, h: (0, 0))],
        out_specs=[e_blk, e_blk, th_blk],
        out_shape=[e_shape, e_shape, jax.ShapeDtypeStruct((heads, 1, t), _F32)],
        compiler_params=_cp("parallel", "parallel"),
        name="peer_route",
    )(q, k1, k2)
    tb = _tile(t, 512, 128)
    n_sub = 8
    eb = n_sub * nkeys
    xb = x.astype(_BF16)
    return pl.pallas_call(
        functools.partial(_peer_dense_kernel, heads=heads, nkeys=nkeys),
        grid=(t // tb, nkeys // n_sub),
        in_specs=[pl.BlockSpec((tb, d), lambda i, e: (i, 0)),
                  pl.BlockSpec((eb, d), lambda i, e: (e, 0)),
                  pl.BlockSpec((d, eb), lambda i, e: (0, e)),
                  pl.BlockSpec((heads, n_sub, tb), lambda i, e: (0, e, i)),
                  pl.BlockSpec((heads, nkeys, tb), lambda i, e: (0, 0, i)),
                  pl.BlockSpec((heads, 1, tb), lambda i, e: (0, 0, i))],
        out_specs=pl.BlockSpec((d, tb), lambda i, e: (0, i)),
        out_shape=jax.ShapeDtypeStruct((d, t), _F32),
        scratch_shapes=[pltpu.VMEM((eb, tb), _F32), pltpu.VMEM((eb, tb), _BF16)],
        compiler_params=_cp("parallel", "arbitrary"),
        name="peer_dense",
    )(xb, u_bf, vt_bf, e1, e2, th)


def _hyena_na_mixer(h_lat, h_ctx, mod_l, mod_c, w_in, conv_w, conv_b, fparams, skip, rpb, w_out):
    split = conv_w.shape[1]
    wb = w_in.astype(_BF16)
    w_hy, w_na = wb[:, :split], wb[:, split:]
    outs = []
    qkvs = []
    for hh, (sc, sh) in ((h_lat, mod_l), (h_ctx, mod_c)):
        p_hy = _mm(hh, w_hy, scale=sc, shift=sh, name="hy_proj")
        qkvs.append(_mm(hh, w_na, scale=sc, shift=sh, out_dtype=_BF16, name="na_proj"))
        outs.append(_hyena(p_hy, conv_w, conv_b, fparams, skip))
    na_lat = _natten(qkvs[0], qkvs[1], rpb)
    na_ctx = _ctx_attn(qkvs[1])
    wo = w_out.astype(_BF16)
    y_lat = _mm(jnp.concatenate([outs[0], na_lat], axis=-1), wo, name="ab_out")
    y_ctx = _mm(jnp.concatenate([outs[1], na_ctx], axis=-1), wo, name="ab_out")
    return y_lat, y_ctx


def kernel(x, c, ctx, c_ctx, ada_w, ada_b, ln_w, ln_b, ab_w_in, hy_conv_w, hy_conv_b, hy_fw1, hy_fb1, hy_fw2, hy_fb2, hy_fw3, hy_fb3, hy_fw4, hy_skip, na_rpb, ab_w_out, ml_w_in, ml_b_gates, ml_conv_w, ml_conv_b, ml_norm_w, ml_w_out, peer_wq, peer_k1, peer_k2, peer_u, peer_v):
    bsz, seq_len, d = x.shape
    depth = ada_w.shape[0]
    assert bsz == 1 and depth == 2, "one even (Hyena/attention) layer followed by one last odd (mLSTM) layer"
    alpha = (2 * depth) ** 0.25
    h_lat, h_ctx = x[0], ctx[0]
    cvec = jnp.zeros((8, d), _F32).at[0].set(c[0]).at[1].set(c_ctx)
    row = lambda a: a.reshape(1, d)
    for layer in range(depth):
        last = layer == depth - 1
        mod = _mm(cvec, ada_w[layer], pre="silu", bias=ada_b[layer].reshape(1, 6 * d), passes=3, name="ada_mod")
        sh_l, sc_l, g_l, shf_l, scf_l, gf_l = [mod[0:1, i * d:(i + 1) * d] for i in range(6)]
        sh_c, sc_c, g_c, shf_c, scf_c, gf_c = [mod[1:2, i * d:(i + 1) * d] for i in range(6)]
        if layer % 2 == 0:
            fparams = (hy_fw1[0], hy_fb1[0], hy_fw2[0], hy_fb2[0], hy_fw3[0], hy_fb3[0], hy_fw4[0])
            y_lat, y_ctx = _hyena_na_mixer(h_lat, h_ctx, (sc_l, sh_l), (sc_c, sh_c), ab_w_in[0], hy_conv_w[0],
                                           hy_conv_b[0], fparams, hy_skip[0], na_rpb[0], ab_w_out[0])
        else:
            y_lat = _mlstm_mixer(h_lat, h_ctx, (sc_l, sh_l), (sc_c, sh_c), ml_w_in[0], ml_b_gates[0], ml_conv_w[0],
                                 ml_conv_b[0], ml_norm_w[0], ml_w_out[0])
            y_ctx = None
        lw0, lb0, lw1, lb1 = row(ln_w[layer, 0]), row(ln_b[layer, 0]), row(ln_w[layer, 1]), row(ln_b[layer, 1])
        u_bf = peer_u[layer].astype(_BF16)
        vt_bf = peer_v[layer].astype(_BF16).T
        peer = lambda v: _peer(v, peer_wq[layer], peer_k1[layer], peer_k2[layer], u_bf, vt_bf).T
        h_lat, v_lat = _ln_res(h_lat, y_lat, g_l, lw0, lb0, scf_l, shf_l, alpha=alpha, with_mod=True)
        h_lat, _ = _ln_res(h_lat, peer(v_lat), gf_l, lw1, lb1, scf_l, shf_l, alpha=alpha, with_mod=False)
        if not last:
            h_ctx, v_ctx = _ln_res(h_ctx, y_ctx, g_c, lw0, lb0, scf_c, shf_c, alpha=alpha, with_mod=True)
            h_ctx, _ = _ln_res(h_ctx, peer(v_ctx), gf_c, lw1, lb1, scf_c, shf_c, alpha=alpha, with_mod=False)
    return h_lat[None]
```

```python
import functools
import math

import numpy as np
import jax
import jax.numpy as jnp
from jax import lax
from jax.experimental import pallas as pl
from jax.experimental.pallas import tpu as pltpu

_F32 = jnp.float32
_BF16 = jnp.bfloat16

GRID_W = 64
HY_ORDER = 2
HY_EMB = 33
HY_DECAY_TARGET = 1e-2
HY_FAST_PCT = 0.3
HY_SLOW_PCT = 1.5
NA_HEADS = 8
NA_WIN_ROWS = 8
NA_WIN_COLS = 16
ML_HEADS = 8
ML_TILE = 256
ML_M0 = -1e30
ROPE_BASE = 10000.0
PEER_HEADS = 8
PEER_NKEYS = 128
PEER_TOPK = 16
LN_EPS = 1e-6

_VMEM_LIMIT = 52 * 1024 * 1024
_NEG = -1e30
_FFT_DT = jnp.bfloat16

_NT = (((1,), (1,)), ((), ()))
_TN = (((0,), (0,)), ((), ()))


def _cp(*sem):
    return pltpu.CompilerParams(dimension_semantics=sem, vmem_limit_bytes=_VMEM_LIMIT)


def _tile(n, cap, align):
    t = (min(cap, n) // align) * align
    while t >= align:
        if n % t == 0:
            return t
        t -= align
    return n


def _split(x):
    hi = x.astype(_BF16)
    lo = (x.astype(_F32) - hi.astype(_F32)).astype(_BF16)
    return hi, lo


def _dot(a, b, passes=1, dims=None):
    dn = dims or (((a.ndim - 1,), (0,)), ((), ()))
    f = lambda x, y: lax.dot_general(x, y, dn, preferred_element_type=_F32)
    if passes == 1:
        return f(a.astype(_BF16), b.astype(_BF16))
    ah, al = _split(a)
    bh, bl = _split(b)
    return f(ah, bh) + (f(ah, bl) + f(al, bh))


def _sigmoid(x):
    return 1.0 / (1.0 + jnp.exp(-x))


def _mm_kernel(*refs, pre, has_bias, passes):
    it = iter(refs)
    a_ref = next(it)
    if pre == "mod":
        sc_ref, sh_ref = next(it), next(it)
    b_ref = next(it)
    bias_ref = next(it) if has_bias else None
    o_ref = next(it)
    a = a_ref[...]
    if pre == "mod":
        a = a.astype(_F32) * (1.0 + sc_ref[...]) + sh_ref[...]
    elif pre == "silu":
        a = a * _sigmoid(a)
    acc = _dot(a, b_ref[...], passes)
    if has_bias:
        acc = acc + bias_ref[...]
    o_ref[...] = acc.astype(o_ref.dtype)


def _mm(a, b, *, scale=None, shift=None, pre=None, bias=None, out_dtype=_F32, passes=1, name="mm"):
    m, k = a.shape
    n = b.shape[1]
    if scale is not None:
        pre = "mod"
    tile_bytes = 4 * 1024 * 1024
    tm = _tile(m, max(8, min(1024, 2 * tile_bytes // (k * a.dtype.itemsize))), 8)
    tn = _tile(n, max(128, min(8192, tile_bytes // (k * b.dtype.itemsize), tile_bytes // (tm * 4))), 128)
    in_specs = [pl.BlockSpec((tm, k), lambda i, j: (i, 0))]
    args = [a]
    if pre == "mod":
        in_specs += [pl.BlockSpec((1, k), lambda i, j: (0, 0))] * 2
        args += [scale, shift]
    in_specs.append(pl.BlockSpec((k, tn), lambda i, j: (0, j)))
    args.append(b)
    if bias is not None:
        in_specs.append(pl.BlockSpec((1, tn), lambda i, j: (0, j)))
        args.append(bias)
    return pl.pallas_call(
        functools.partial(_mm_kernel, pre=pre, has_bias=bias is not None, passes=passes),
        grid=(m // tm, n // tn),
        in_specs=in_specs,
        out_specs=pl.BlockSpec((tm, tn), lambda i, j: (i, j)),
        out_shape=jax.ShapeDtypeStruct((m, n), out_dtype),
        compiler_params=_cp("parallel", "parallel"),
        name=name,
    )(*args)


def _ln_res_kernel(h_ref, y_ref, g_ref, w_ref, b_ref, sc_ref, sh_ref, *o_refs, alpha, with_mod, y_transposed):
    y = y_ref[...].T if y_transposed else y_ref[...]
    z = alpha * h_ref[...] + g_ref[...] * y
    mu = jnp.mean(z, axis=-1, keepdims=True)
    zc = z - mu
    var = jnp.mean(zc * zc, axis=-1, keepdims=True)
    hn = zc * lax.rsqrt(var + LN_EPS) * w_ref[...] + b_ref[...]
    o_refs[0][...] = hn
    if with_mod:
        o_refs[1][...] = hn * (1.0 + sc_ref[...]) + sh_ref[...]


def _ln_res(h, y, g, w, b, sc, sh, *, alpha, with_mod, y_transposed=False):
    m, d = h.shape
    tr = _tile(m, 256, 128 if y_transposed else 8)
    row = pl.BlockSpec((tr, d), lambda i: (i, 0))
    y_spec = pl.BlockSpec((d, tr), lambda i: (0, i)) if y_transposed else row
    vec = pl.BlockSpec((1, d), lambda i: (0, 0))
    n_out = 2 if with_mod else 1
    out = pl.pallas_call(
        functools.partial(_ln_res_kernel, alpha=alpha, with_mod=with_mod, y_transposed=y_transposed),
        grid=(m // tr,),
        in_specs=[row, y_spec, vec, vec, vec, vec, vec],
        out_specs=[row] * n_out,
        out_shape=[jax.ShapeDtypeStruct((m, d), _F32)] * n_out,
        compiler_params=_cp("parallel"),
        name="ln_res",
    )(h, y, g, w, b, sc, sh)
    return out if with_mod else (out[0], None)


def _conv3_kernel(*refs, n_out, act, rope, out_scales):
    it = iter(refs)
    xp_ref, x_ref, xn_ref, w_ref, b_ref = (next(it) for _ in range(5))
    if rope:
        cos_ref, sin_ref = next(it), next(it)
    o_refs = [next(it) for _ in range(n_out)]
    i = pl.program_id(0)
    x = x_ref[...]
    tr = x.shape[0]
    rows = lax.broadcasted_iota(jnp.int32, x.shape, 0)
    prev_row = jnp.where(i > 0, xp_ref[7:8, :], 0.0)
    next_row = jnp.where(i < pl.num_programs(0) - 1, xn_ref[0:1, :], 0.0)
    x_m1 = jnp.where(rows == 0, prev_row, pltpu.roll(x, 1, axis=0))
    x_p1 = jnp.where(rows == tr - 1, next_row, pltpu.roll(x, tr - 1, axis=0))
    y = b_ref[...] + x_m1 * w_ref[0:1, :] + x * w_ref[1:2, :] + x_p1 * w_ref[2:3, :]
    if act:
        y = y * _sigmoid(y)
    wo = y.shape[1] // n_out
    for o in range(n_out):
        yo = y[:, o * wo:(o + 1) * wo]
        if rope:
            cos, sin = cos_ref[...], sin_ref[...]
            lane = lax.broadcasted_iota(jnp.int32, cos.shape, 1)
            first = (lane % 64) < 32
            parts = []
            for hh in range(wo // 128):
                yh = yo[:, hh * 128:(hh + 1) * 128]
                sw = jnp.where(first, pltpu.roll(yh, 96, axis=1), pltpu.roll(yh, 32, axis=1))
                parts.append(yh * cos + sw * sin)
            yo = jnp.concatenate(parts, axis=1)
        if out_scales[o] != 1.0:
            yo = yo * out_scales[o]
        o_refs[o][...] = yo.astype(o_refs[o].dtype)


def _conv3(x, w, b, *, n_out, out_dtype, act=False, rope=None, out_scales=None):
    l, cx = x.shape
    tr = _tile(l, 256, 8)
    nb8 = l // 8
    out_scales = out_scales or (1.0,) * n_out
    in_specs = [
        pl.BlockSpec((8, cx), lambda i: (jnp.maximum(i * (tr // 8) - 1, 0), 0)),
        pl.BlockSpec((tr, cx), lambda i: (i, 0)),
        pl.BlockSpec((8, cx), lambda i: (jnp.minimum((i + 1) * (tr // 8), nb8 - 1), 0)),
        pl.BlockSpec((3, cx), lambda i: (0, 0)),
        pl.BlockSpec((1, cx), lambda i: (0, 0)),
    ]
    args = [x, x, x, w, b.reshape(1, cx)]
    if rope is not None:
        in_specs += [pl.BlockSpec((tr, 128), lambda i: (i, 0))] * 2
        args += list(rope)
    wo = cx // n_out
    return pl.pallas_call(
        functools.partial(_conv3_kernel, n_out=n_out, act=act, rope=rope is not None, out_scales=out_scales),
        grid=(l // tr,),
        in_specs=in_specs,
        out_specs=[pl.BlockSpec((tr, wo), lambda i: (i, 0))] * n_out,
        out_shape=[jax.ShapeDtypeStruct((l, wo), out_dtype)] * n_out,
        compiler_params=_cp("parallel"),
        name="conv3",
    )(*args)


def _hy_filter_kernel(f_ref, t_ref, w1_ref, b1_ref, w2_ref, b2_ref, w3_ref, b3_ref, w4_ref, d_ref, o_ref):
    h = jnp.sin(_dot(f_ref[...], w1_ref[...], 3) + b1_ref[...])
    h = jnp.sin(_dot(h, w2_ref[...], 3) + b2_ref[...])
    h = jnp.sin(_dot(h, w3_ref[...], 3) + b3_ref[...])
    taps = _dot(h, w4_ref[...], 3)
    t = t_ref[...]
    window = jnp.exp(-t * d_ref[...])
    c = window.shape[1]
    tl = t.shape[0]
    row = pl.program_id(0) * tl + lax.broadcasted_iota(jnp.int32, (tl, 1), 0)
    not_first = jnp.where(row == 0, 0.0, 1.0)
    n_grp = taps.shape[1] // c
    for g in range(n_grp):
        blk = taps[:, g * c:(g + 1) * c] * window
        if g >= n_grp // 2:
            blk = blk * not_first
        o_ref[:, g * c:(g + 1) * c] = blk.astype(o_ref.dtype)


def _hy_filter(seq_len, fw1, fb1, fw2, fb2, fw3, fb3, fw4, width):
    hid = fw1.shape[1]
    t = jnp.linspace(0.0, 1.0, seq_len, dtype=_F32)[:, None]
    omega = 2.0 * math.pi * jnp.arange(seq_len, dtype=_F32) / seq_len
    bands = (HY_EMB - 1) // 2
    freqs = jnp.linspace(1e-4, bands - 1, bands, dtype=_F32)
    ang = omega[:, None] * freqs[None, :]
    kpad = 64
    feats = jnp.concatenate([t, jnp.cos(ang), -jnp.sin(ang), jnp.zeros((seq_len, kpad - HY_EMB), _F32)], axis=-1)
    w1 = jnp.concatenate([fw1, jnp.zeros((kpad - HY_EMB, hid), _F32)], axis=0)
    max_decay = math.log(HY_DECAY_TARGET) / HY_FAST_PCT
    min_decay = math.log(HY_DECAY_TARGET) / HY_SLOW_PCT
    deltas = jnp.abs(jnp.linspace(min_decay, max_decay, width, dtype=_F32))[None, :]
    n_cols = fw4.shape[1]
    tl = _tile(seq_len, 256, 8)
    full = lambda shape: pl.BlockSpec(shape, lambda i: (0,) * len(shape))
    return pl.pallas_call(
        _hy_filter_kernel,
        grid=(seq_len // tl,),
        in_specs=[pl.BlockSpec((tl, kpad), lambda i: (i, 0)), pl.BlockSpec((tl, 1), lambda i: (i, 0)),
                  full((kpad, hid)), full((1, hid)), full((hid, hid)), full((1, hid)), full((hid, hid)), full((1, hid)),
                  full((hid, n_cols)), full((1, width))],
        out_specs=pl.BlockSpec((tl, n_cols), lambda i: (i, 0)),
        out_shape=jax.ShapeDtypeStruct((seq_len, n_cols), _BF16),
        compiler_params=_cp("parallel"),
        name="hy_filter",
    )(feats, t, w1, fb1.reshape(1, hid), fw2, fb2.reshape(1, hid), fw3, fb3.reshape(1, hid), fw4, deltas)


def _fft_consts(n1, n2):
    n = n1 * n2
    k1 = np.arange(n1)[:, None]
    j1 = np.arange(n1 // 2)[None, :]
    ang1 = 2.0 * np.pi * k1 * j1 / n1
    f1_fwd = np.concatenate([np.cos(ang1), -np.sin(ang1)], axis=0)
    f1_inv = np.concatenate([np.cos(ang1).T, -np.sin(ang1).T], axis=1) / n
    k2 = np.arange(n2)[:, None]
    j2 = np.arange(n2)[None, :]
    ang2 = 2.0 * np.pi * k2 * j2 / n2
    f2 = np.stack([np.cos(ang2), -np.sin(ang2)])
    angt = 2.0 * np.pi * np.arange(n1)[:, None] * np.arange(n2)[None, :] / n
    tw = np.stack([np.cos(angt), -np.sin(angt)], axis=1)
    angf = 2.0 * np.pi * k1 * np.arange(n1)[None, :] / n1
    f1_full = np.concatenate([np.cos(angf), -np.sin(angf)], axis=0)
    as32 = lambda a: jnp.asarray(a, dtype=_F32)
    return as32(f1_fwd), as32(f1_inv), as32(f2), as32(tw), as32(np.transpose(tw, (0, 2, 1))), as32(f1_full)


def _cmm(f_stack, xr, xi):
    n = f_stack.shape[0] // 2
    p1 = jnp.dot(f_stack, xr.astype(_BF16), preferred_element_type=_F32)
    p2 = jnp.dot(f_stack, xi.astype(_BF16), preferred_element_type=_F32)
    return p1[:n] - p2[n:], p2[:n] + p1[n:]


def _twiddled_dft(f2_ref, tw_ref):
    f2r, f2i = f2_ref[0], f2_ref[1]
    tr, ti = tw_ref[0, 0:1, :], tw_ref[0, 1:2, :]
    return jnp.concatenate([f2r * tr - f2i * ti, f2r * ti + f2i * tr], axis=0).astype(_BF16)


def _fft_spec_kernel(a_ref, f2_ref, tw_ref, o_ref):
    fw = _twiddled_dft(f2_ref, tw_ref)
    hr, hi = _cmm(fw, a_ref[0, 0], a_ref[1, 0])
    o_ref[0, 0, 0] = hr.astype(o_ref.dtype)
    o_ref[0, 1, 0] = hi.astype(o_ref.dtype)


def _fft_mid_kernel(a_ref, h_ref, f2_ref, tw_ref, twc_ref, o_ref):
    fw = _twiddled_dft(f2_ref, tw_ref)
    br, bi = _cmm(fw, a_ref[0, 0], a_ref[1, 0])
    hr, hi = h_ref[0, 0, 0].astype(_F32), h_ref[0, 1, 0].astype(_F32)
    yr = (br * hr - bi * hi).astype(_BF16)
    yi = (br * hi + bi * hr).astype(_BF16)
    f2r, f2i = f2_ref[0], f2_ref[1]
    tcr, tci = twc_ref[0, :, 0:1], twc_ref[0, :, 1:2]
    g = jnp.concatenate([tcr * f2r - tci * f2i, -(tcr * f2i + tci * f2r)], axis=0).astype(_BF16)
    zr, zi = _cmm(g, yr, yi)
    o_ref[0, 0] = zr.astype(o_ref.dtype)
    o_ref[1, 0] = zi.astype(o_ref.dtype)


def _fft_shape(l_pad):
    n2 = 256 if l_pad >= 8192 else 64
    n1 = 2 * l_pad // n2
    return n1, n2


def _hy_spectra(filt, width, consts, n1, n2):
    f1_fwd, _, f2, tw, _, f1_full = consts
    half = filt.shape[1] // 2
    taps = jnp.concatenate([filt[:, :half], jnp.roll(jnp.flip(filt[:, half:], axis=0), 1, axis=0)], axis=0)
    a = _mm(f1_full, taps.reshape(n1, n2 * half), out_dtype=_FFT_DT, name="fft_a_filt")
    a = a.reshape(2, n1, n2, half)
    return pl.pallas_call(
        _fft_spec_kernel,
        grid=(n1, HY_ORDER),
        in_specs=[pl.BlockSpec((2, 1, n2, width), lambda k, o: (0, k, 0, o)),
                  pl.BlockSpec((2, n2, n2), lambda k, o: (0, 0, 0)),
                  pl.BlockSpec((1, 2, n2), lambda k, o: (k, 0, 0))],
        out_specs=pl.BlockSpec((1, 2, 1, n2, width), lambda k, o: (o, 0, k, 0, 0)),
        out_shape=jax.ShapeDtypeStruct((HY_ORDER, 2, n1, n2, width), _FFT_DT),
        compiler_params=_cp("parallel", "parallel"),
        name="fft_spec",
    )(a, f2, tw)


def _long_conv(z, spec, order, consts, n1, n2):
    f1_fwd, f1_inv, f2, tw, twc, _ = consts
    l_pad, c = z.shape
    a = _mm(f1_fwd, z.reshape(n1 // 2, n2 * c), out_dtype=_FFT_DT, name="fft_a")
    a = a.reshape(2, n1, n2, c)
    blk = pl.BlockSpec((2, 1, n2, c), lambda k: (0, k, 0, 0))
    mid = pl.pallas_call(
        _fft_mid_kernel,
        grid=(n1,),
        in_specs=[blk, pl.BlockSpec((1, 2, 1, n2, c), lambda k: (order, 0, k, 0, 0)),
                  pl.BlockSpec((2, n2, n2), lambda k: (0, 0, 0)),
                  pl.BlockSpec((1, 2, n2), lambda k: (k, 0, 0)), pl.BlockSpec((1, n2, 2), lambda k: (k, 0, 0))],
        out_specs=blk,
        out_shape=jax.ShapeDtypeStruct((2, n1, n2, c), _FFT_DT),
        compiler_params=_cp("parallel"),
        name="fft_mid",
    )(a, spec, f2, tw, twc)
    y = _mm(f1_inv, mid.reshape(2 * n1, n2 * c), name="fft_c")
    return y.reshape(l_pad, c)


def _hy_gate_kernel(y_ref, z_ref, s_ref, g_ref, o_ref):
    o_ref[...] = (g_ref[...] * (y_ref[...] + z_ref[...] * s_ref[...])).astype(o_ref.dtype)


def _hy_gate(y, z, skip, gate, out_dtype):
    l, c = y.shape
    tr = _tile(l, 512, 8)
    row = pl.BlockSpec((tr, c), lambda i: (i, 0))
    return pl.pallas_call(
        _hy_gate_kernel,
        grid=(l // tr,),
        in_specs=[row, row, pl.BlockSpec((1, c), lambda i: (0, 0)), row],
        out_specs=row,
        out_shape=jax.ShapeDtypeStruct((l, c), out_dtype),
        compiler_params=_cp("parallel"),
        name="hy_gate",
    )(y, z, skip.reshape(1, c), gate)


def _hyena(p, conv_w, conv_b, fparams, skip):
    l = p.shape[0]
    width = p.shape[1] // (HY_ORDER + 1)
    l_pad = max(l, 1024)
    n1, n2 = _fft_shape(l_pad)
    consts = _fft_consts(n1, n2)
    filt = _hy_filter(l, *fparams, width)
    parts = _conv3(p, conv_w, conv_b, n_out=HY_ORDER + 1, out_dtype=_F32)
    if l_pad != l:
        pad = lambda a: jnp.pad(a, ((0, l_pad - l), (0, 0)))
        filt = pad(filt)
        parts = [pad(a) for a in parts]
    spec = _hy_spectra(filt, width, consts, n1, n2)
    z = parts[0]
    for o in range(HY_ORDER):
        y = _long_conv(z, spec, o, consts, n1, n2)
        z = _hy_gate(y, z, skip[o], parts[o + 1], _F32 if o + 1 < HY_ORDER else _BF16)
    return z[:l]


def _na_bias_table(rpb, w):
    kr, kc = NA_WIN_ROWS, NA_WIN_COLS
    cols = np.arange(w)
    start = np.clip(cols - kc // 2, 0, w - kc)
    kcol = np.arange(w)[None, :]
    in_win = (kcol >= start[:, None]) & (kcol < start[:, None] + kc)
    rp = jnp.pad(rpb, ((0, 0), (0, 0), (w - 1, w)))
    by_q = jnp.stack([rp[:, :, kc - 1 + w - 1 - q:kc - 1 + 2 * w - 1 - q] for q in range(w)], axis=2)
    by_q = jnp.where(in_win[None, None], by_q, _NEG)
    t = jnp.stack([jnp.concatenate([by_q[:, i - d + kr - 1] for i in range(kr)], axis=-1) for d in range(kr)], axis=1)
    return t


def _natten_kernel(q_ref, k_ref, v_ref, kc_ref, vc_ref, bias_ref, o_ref, *, rows, w, scale):
    qb = pl.program_id(1)
    kr = NA_WIN_ROWS
    kc, vc = kc_ref[...], vc_ref[...]
    for a in range(kr):
        r = qb * kr + a
        r0 = jnp.clip(r - kr // 2, 0, rows - kr)
        d = r - r0
        start = pl.multiple_of(r0 * w, w)
        kwin = k_ref[pl.ds(start, kr * w), :]
        vwin = v_ref[pl.ds(start, kr * w), :]
        qa = q_ref[a * w:(a + 1) * w, :]
        s_lat = _dot(qa, kwin, dims=_NT) * scale + bias_ref[0, d]
        s_ctx = _dot(qa, kc, dims=_NT) * scale
        m = jnp.maximum(jnp.max(s_lat, axis=-1, keepdims=True), jnp.max(s_ctx, axis=-1, keepdims=True))
        p_lat = jnp.exp(s_lat - m)
        p_ctx = jnp.exp(s_ctx - m)
        den = jnp.sum(p_lat, axis=-1, keepdims=True) + jnp.sum(p_ctx, axis=-1, keepdims=True)
        o = _dot(p_lat, vwin) + _dot(p_ctx, vc)
        o_ref[a * w:(a + 1) * w, :] = (o / den).astype(o_ref.dtype)


def _natten(qkv, kv_ctx, rpb):
    l = qkv.shape[0]
    h = NA_HEADS
    hd = qkv.shape[1] // (3 * h)
    w = GRID_W
    rows = l // w
    kr = NA_WIN_ROWS
    assert rows % kr == 0 and rows >= kr and hd % 128 == 0
    lc = kv_ctx.shape[0]
    bias = _na_bias_table(rpb, w)
    tq = kr * w
    return pl.pallas_call(
        functools.partial(_natten_kernel, rows=rows, w=w, scale=hd ** -0.5),
        grid=(h, rows // kr),
        in_specs=[pl.BlockSpec((tq, hd), lambda hh, qb: (qb, hh)),
                  pl.BlockSpec((l, hd), lambda hh, qb: (0, h + hh)),
                  pl.BlockSpec((l, hd), lambda hh, qb: (0, 2 * h + hh)),
                  pl.BlockSpec((lc, hd), lambda hh, qb: (0, h + hh)),
                  pl.BlockSpec((lc, hd), lambda hh, qb: (0, 2 * h + hh)),
                  pl.BlockSpec((1, kr, w, kr * w), lambda hh, qb: (hh, 0, 0, 0))],
        out_specs=pl.BlockSpec((tq, hd), lambda hh, qb: (qb, hh)),
        out_shape=jax.ShapeDtypeStruct((l, h * hd), _BF16),
        compiler_params=_cp("parallel", "parallel"),
        name="natten",
    )(qkv, qkv, qkv, kv_ctx, kv_ctx, bias)


def _ctx_attn_kernel(q_ref, k_ref, v_ref, o_ref, *, scale):
    s = _dot(q_ref[...], k_ref[...], dims=_NT) * scale
    p = jnp.exp(s - jnp.max(s, axis=-1, keepdims=True))
    den = jnp.sum(p, axis=-1, keepdims=True)
    o_ref[...] = (_dot(p, v_ref[...]) / den).astype(o_ref.dtype)


def _ctx_attn(qkv):
    lc = qkv.shape[0]
    h = NA_HEADS
    hd = qkv.shape[1] // (3 * h)
    blk = lambda off: pl.BlockSpec((lc, hd), lambda hh: (0, off + hh))
    return pl.pallas_call(
        functools.partial(_ctx_attn_kernel, scale=hd ** -0.5),
        grid=(h,),
        in_specs=[blk(0), blk(h), blk(2 * h)],
        out_specs=blk(0),
        out_shape=jax.ShapeDtypeStruct((lc, h * hd), _BF16),
        compiler_params=_cp("parallel"),
        name="ctx_attn",
    )(qkv, qkv, qkv)


def _ml_gates_kernel(g_ref, b_ref, o_ref, *, heads):
    g = g_ref[...] + b_ref[...]
    logsig = jnp.minimum(g, 0.0) - jnp.log(1.0 + jnp.exp(-jnp.abs(g)))
    lane = lax.broadcasted_iota(jnp.int32, g.shape, 1)
    is_forget = (lane // heads) % 2 == 1
    o_ref[...] = jnp.where(is_forget, logsig, g)


def _ml_gates(g_raw, b_gates):
    l, n = g_raw.shape
    tr = _tile(l, 1024, 8)
    bias = jnp.pad(b_gates, (0, n - b_gates.shape[0])).reshape(1, n)
    return pl.pallas_call(
        functools.partial(_ml_gates_kernel, heads=ML_HEADS),
        grid=(l // tr,),
        in_specs=[pl.BlockSpec((tr, n), lambda i: (i, 0)), pl.BlockSpec((1, n), lambda i: (0, 0))],
        out_specs=pl.BlockSpec((tr, n), lambda i: (i, 0)),
        out_shape=jax.ShapeDtypeStruct((l, n), _F32),
        compiler_params=_cp("parallel"),
        name="ml_gates",
    )(g_raw, bias)


def _ml_scan_kernel(*refs, heads, dk, dv, reverse, with_h):
    it = iter(refs)
    if with_h:
        q_ref = next(it)
    k_ref, v_ref, g_ref, gt_ref, c0_ref, n0_ref, m0_ref = (next(it) for _ in range(7))
    if with_h:
        h_ref = next(it)
    c_ref, n_ref, m_ref = next(it), next(it), next(it)

    @pl.when(pl.program_id(0) == 0)
    def _():
        c_ref[...] = c0_ref[...]
        n_ref[...] = n0_ref[...]
        m_ref[...] = m0_ref[...]

    t = k_ref.shape[0]
    ti = lax.broadcasted_iota(jnp.int32, (t, t), 0)
    si = lax.broadcasted_iota(jnp.int32, (t, t), 1)
    seen = (si >= ti) if reverse else (si <= ti)
    seen_t = (ti >= si) if reverse else (ti <= si)
    g = g_ref[...]
    gt = gt_ref[0]
    c0 = 2 * heads if reverse else 0
    for h in range(heads):
        i_col = g[:, c0 + h:c0 + h + 1]
        f_col = g[:, c0 + heads + h:c0 + heads + h + 1]
        i_row = gt[c0 + h:c0 + h + 1, :]
        f_row = gt[c0 + heads + h:c0 + heads + h + 1, :]
        b_col = jnp.sum(jnp.where(seen, f_row, 0.0), axis=1, keepdims=True)
        b_row = jnp.sum(jnp.where(seen_t, f_col, 0.0), axis=0, keepdims=True)
        b_end = jnp.sum(f_row, axis=1, keepdims=True)
        m_prev = m_ref[h][:, 0:1]
        w_log_row = b_end - b_row + i_row
        w_log_col = b_end - b_col + i_col
        m_new = jnp.maximum(b_end + m_prev, jnp.max(w_log_row, axis=1, keepdims=True))
        decay = jnp.exp(b_end + m_prev - m_new)
        w_col = jnp.exp(w_log_col - m_new)
        kh = k_ref[:, h * dk:(h + 1) * dk]
        vh = v_ref[:, h * dv:(h + 1) * dv]
        ct = c_ref[h]
        nv = n_ref[h]
        if with_h:
            qh = q_ref[:, h * dk:(h + 1) * dk]
            log_d = jnp.where(seen, b_col - b_row + i_row, -jnp.inf)
            m_in = b_col + m_prev
            m_t = jnp.maximum(m_in, jnp.max(log_d, axis=1, keepdims=True))
            sc = _dot(qh, kh, dims=_NT) * jnp.exp(log_d - m_t)
            g_in = jnp.exp(m_in - m_t)
            num = _dot(sc, vh) + _dot(qh, ct) * g_in
            den = jnp.sum(sc, axis=1, keepdims=True) + jnp.sum(qh.astype(_F32) * nv, axis=1, keepdims=True) * g_in
            h_ref[:, h * dv:(h + 1) * dv] = num / jnp.maximum(jnp.abs(den), jnp.exp(-m_t))
        vw = vh.astype(_F32) * w_col
        c_ref[h] = decay * ct + _dot(kh, vw, dims=_TN)
        n_ref[h] = decay * nv + jnp.sum(kh.astype(_F32) * w_col, axis=0, keepdims=True)
        m_ref[h] = jnp.broadcast_to(m_new, m_ref.shape[1:])


def _ml_scan(q, k, v, gates, state, *, reverse, dk, dv):
    l = k.shape[0]
    heads = ML_HEADS
    t = min(ML_TILE, l)
    assert l % t == 0
    nc = l // t
    with_h = q is not None
    gt = jnp.transpose(gates.reshape(nc, t, gates.shape[1]), (0, 2, 1))
    cidx = (lambda j: nc - 1 - j) if reverse else (lambda j: j)
    row = lambda width: pl.BlockSpec((t, width), lambda j: (cidx(j), 0))
    st_specs = [pl.BlockSpec((heads, dk, dv), lambda j: (0, 0, 0)), pl.BlockSpec((heads, 1, dk), lambda j: (0, 0, 0)),
                pl.BlockSpec((heads, 1, 128), lambda j: (0, 0, 0))]
    st_shapes = [jax.ShapeDtypeStruct((heads, dk, dv), _F32), jax.ShapeDtypeStruct((heads, 1, dk), _F32),
                 jax.ShapeDtypeStruct((heads, 1, 128), _F32)]
    in_specs = ([row(heads * dk)] if with_h else []) + [
        row(heads * dk), row(heads * dv), row(gates.shape[1]),
        pl.BlockSpec((1, gates.shape[1], t), lambda j: (cidx(j), 0, 0))] + st_specs
    args = ([q] if with_h else []) + [k, v, gates, gt, *state]
    out_specs = ([row(heads * dv)] if with_h else []) + st_specs
    out_shape = ([jax.ShapeDtypeStruct((l, heads * dv), _F32)] if with_h else []) + st_shapes
    out = pl.pallas_call(
        functools.partial(_ml_scan_kernel, heads=heads, dk=dk, dv=dv, reverse=reverse, with_h=with_h),
        grid=(nc,),
        in_specs=in_specs,
        out_specs=out_specs,
        out_shape=out_shape,
        compiler_params=_cp("arbitrary"),
        name="ml_scan",
    )(*args)
    return (out[0], tuple(out[1:])) if with_h else (None, tuple(out))


def _ml_merge_kernel(hf_ref, hb_ref, o_ref, w_ref, out_ref, *, heads, dv):
    for h in range(heads):
        sl = slice(h * dv, (h + 1) * dv)
        x = hf_ref[:, sl] + hb_ref[:, sl]
        mu = jnp.mean(x, axis=-1, keepdims=True)
        xc = x - mu
        var = jnp.mean(xc * xc, axis=-1, keepdims=True)
        hn = xc * lax.rsqrt(var + LN_EPS) * w_ref[:, sl]
        out_ref[:, sl] = (hn * _sigmoid(o_ref[:, sl])).astype(out_ref.dtype)


def _ml_merge(hf, hb, o, norm_w, dv):
    l, n = hf.shape
    tr = _tile(l, 256, 8)
    row = pl.BlockSpec((tr, n), lambda i: (i, 0))
    return pl.pallas_call(
        functools.partial(_ml_merge_kernel, heads=ML_HEADS, dv=dv),
        grid=(l // tr,),
        in_specs=[row, row, row, pl.BlockSpec((1, n), lambda i: (0, 0))],
        out_specs=row,
        out_shape=jax.ShapeDtypeStruct((l, n), _BF16),
        compiler_params=_cp("parallel"),
        name="ml_merge",
    )(hf, hb, o, norm_w.reshape(1, n))


def _rope_tables(l):
    quarter = 32
    pos = jnp.arange(l)
    inv = ROPE_BASE ** (-jnp.arange(quarter, dtype=_F32) / quarter)
    ang_r = (pos // GRID_W).astype(_F32)[:, None] * inv[None]
    ang_c = (pos % GRID_W).astype(_F32)[:, None] * inv[None]
    cos = jnp.concatenate([jnp.cos(ang_r)] * 2 + [jnp.cos(ang_c)] * 2, axis=-1)
    sin = jnp.concatenate([-jnp.sin(ang_r), jnp.sin(ang_r), -jnp.sin(ang_c), jnp.sin(ang_c)], axis=-1)
    return cos, sin


def _mlstm_mixer(h_lat, h_ctx, mod_l, mod_c, w_in, b_gates, conv_w, conv_b, norm_w, w_out):
    heads = ML_HEADS
    v_w = w_out.shape[0]
    dv = v_w // heads
    dk = dv // 2
    qk_w = heads * dk
    g0 = 2 * qk_w + 2 * v_w
    n_g = w_in.shape[1] - g0
    assert dk == 128
    l = h_lat.shape[0]
    wb = w_in.astype(_BF16)
    w_g = jnp.pad(w_in[:, g0:], ((0, 0), (0, 128 - n_g)))
    sc_l, sh_l = mod_l
    sc_c, sh_c = mod_c
    qk_l = _mm(h_lat, wb[:, :2 * qk_w], scale=sc_l, shift=sh_l, name="ml_proj_qk")
    v_l = _mm(h_lat, wb[:, 2 * qk_w:2 * qk_w + v_w], scale=sc_l, shift=sh_l, out_dtype=_BF16, name="ml_proj_v")
    o_l = _mm(h_lat, wb[:, 2 * qk_w + v_w:g0], scale=sc_l, shift=sh_l, name="ml_proj_o")
    g_l = _ml_gates(_mm(h_lat, w_g, scale=sc_l, shift=sh_l, passes=3, name="ml_proj_g"), b_gates)
    k_c_raw = _mm(h_ctx, wb[:, qk_w:2 * qk_w], scale=sc_c, shift=sh_c, name="ml_proj_kc")
    v_c = _mm(h_ctx, wb[:, 2 * qk_w:2 * qk_w + v_w], scale=sc_c, shift=sh_c, out_dtype=_BF16, name="ml_proj_vc")
    g_c = _ml_gates(_mm(h_ctx, w_g, scale=sc_c, shift=sh_c, passes=3, name="ml_proj_gc"), b_gates)
    k_scale = dk ** -0.5
    q_l, k_l = _conv3(qk_l, conv_w, conv_b, n_out=2, out_dtype=_BF16, act=True, rope=_rope_tables(l),
                      out_scales=(1.0, k_scale))
    (k_c,) = _conv3(k_c_raw, conv_w[:, qk_w:], conv_b[qk_w:], n_out=1, out_dtype=_BF16, act=True,
                    out_scales=(k_scale,))
    init = (jnp.zeros((heads, dk, dv), _F32), jnp.zeros((heads, 1, dk), _F32), jnp.full((heads, 1, 128), ML_M0, _F32))
    scan = functools.partial(_ml_scan, dk=dk, dv=dv)
    _, st_f = scan(None, k_c, v_c, g_c, init, reverse=False)
    _, st_b = scan(None, k_c, v_c, g_c, init, reverse=True)
    hl_f, _ = scan(q_l, k_l, v_l, g_l, st_f, reverse=False)
    hl_b, _ = scan(q_l, k_l, v_l, g_l, st_b, reverse=True)
    merged = _ml_merge(hl_f, hl_b, o_l, norm_w, dv)
    return _mm(merged, w_out.astype(_BF16), name="ml_out")


_SUBLANES = 8


def _sort_network(n):
    pairs = []
    p = 1
    while p < n:
        k = p
        while k >= 1:
            for j in range(k % p, n - k, 2 * k):
                for i in range(min(k, n - j - k)):
                    if (i + j) // (2 * p) == (i + j + k) // (2 * p):
                        pairs.append((i + j, i + j + k))
            k //= 2
        p *= 2
    return pairs


def _top_desc(x, count):
    n = x.shape[0] // _SUBLANES
    tiles = [x[j * _SUBLANES:(j + 1) * _SUBLANES, :] for j in range(n)]
    for a, b in _sort_network(n):
        tiles[a], tiles[b] = jnp.maximum(tiles[a], tiles[b]), jnp.minimum(tiles[a], tiles[b])
    slot = lax.broadcasted_iota(jnp.int32, tiles[0].shape, 0).astype(_F32)
    vals = []
    for it in range(count):
        m = jnp.max(tiles[0], axis=0, keepdims=True)
        vals.append(m)
        remaining = count - it - 1
        if remaining == 0:
            break
        first = jnp.min(jnp.where(tiles[0] == m, slot, float(_SUBLANES)), axis=0, keepdims=True)
        hit = slot == first
        for j in range(min(remaining, n)):
            below = tiles[j + 1] if j + 1 < n else -jnp.inf
            tiles[j] = jnp.where(hit, below, tiles[j])
    return vals


def _peer_route_kernel(q_ref, k1_ref, k2_ref, e1_ref, e2_ref, th_ref):
    half = k1_ref.shape[1]
    q = q_ref[...]
    s1 = _dot(k1_ref[...], q[:, :half], 3, dims=_NT)
    s2 = _dot(k2_ref[...], q[:, half:], 3, dims=_NT)
    n_top = PEER_TOPK + 1
    v1 = _top_desc(s1, n_top)
    v2 = _top_desc(s2, n_top)
    cands = [v1[a] + v2[b] for a in range(n_top) for b in range(n_top) if (a + 1) * (b + 1) <= n_top]
    pad = 64 - len(cands)
    cand = jnp.concatenate(cands + [jnp.full_like(cands[0], -jnp.inf)] * pad, axis=0)
    top = _top_desc(cand, n_top)
    m = top[0]
    z = top[0] * 0.0
    for j in range(PEER_TOPK):
        z = z + jnp.exp(top[j] - m)
    tau = 0.5 * (top[PEER_TOPK - 1] + top[PEER_TOPK])
    inv_z = 1.0 / z
    e1_ref[0] = jnp.exp(s1 - v1[0])
    e2_ref[0] = jnp.exp(s2 - v2[0]) * inv_z
    th_ref[0] = jnp.exp(tau - m) * inv_z


def _peer_dense_kernel(x_ref, u_ref, vt_ref, e1_ref, e2_ref, th_ref, o_ref, act_ref, a_ref, *, heads, nkeys):
    @pl.when(pl.program_id(1) == 0)
    def _():
        o_ref[...] = jnp.zeros_like(o_ref)

    n_sub = u_ref.shape[0] // nkeys
    n_part = 2
    part = u_ref.shape[0] // n_part
    lanes = 128

    def activations(pt):
        rows = slice(pt * part, (pt + 1) * part)
        act_ref[rows, :] = _dot(u_ref[rows, :], x_ref[...], dims=_NT)

    def gated(i):
        rows = slice(i * nkeys, (i + 1) * nkeys)
        for c in range(x_ref.shape[0] // lanes):
            cols = slice(c * lanes, (c + 1) * lanes)
            gate = None
            for h in range(heads):
                p = e1_ref[h, i:i + 1, cols] * e2_ref[h, :, cols]
                sel = jnp.where(p >= th_ref[h, :, cols], p, 0.0)
                gate = sel if gate is None else gate + sel
            act = act_ref[rows, cols]
            gelu = 0.5 * act * (1.0 + lax.erf(act * (2.0 ** -0.5)))
            a_ref[rows, cols] = (gate * gelu).astype(a_ref.dtype)

    activations(0)
    for pt in range(n_part):
        if pt + 1 < n_part:
            activations(pt + 1)
        for i in range(pt * n_sub // n_part, (pt + 1) * n_sub // n_part):
            gated(i)
        rows = slice(pt * part, (pt + 1) * part)
        o_ref[...] += jnp.dot(vt_ref[:, rows], a_ref[rows, :], preferred_element_type=_F32)


def _peer(x, wq, k1, k2, u_bf, vt_bf):
    t, d = x.shape
    heads, nkeys = PEER_HEADS, PEER_NKEYS
    qdim = wq.shape[1] // heads
    q = _mm(x, wq.astype(_BF16), name="peer_q")
    tt = _tile(t, 512, 128)
    e_shape = jax.ShapeDtypeStruct((heads, nkeys, t), _F32)
    e_blk = pl.BlockSpec((1, nkeys, tt), lambda i, h: (h, 0, i))
    th_blk = pl.BlockSpec((1, 1, tt), lambda i, h: (h, 0, i))
    e1, e2, th = pl.pallas_call(
        _peer_route_kernel,
        grid=(t // tt, heads),
        in_specs=[pl.BlockSpec((tt, qdim), lambda i, h: (i, h)),
                  pl.BlockSpec((nkeys, qdim // 2), lambda i, h: (0, 0)),
                  pl.BlockSpec((nkeys, qdim // 2), lambda i, h: (0, 0))],
        out_specs=[e_blk, e_blk, th_blk],
        out_shape=[e_shape, e_shape, jax.ShapeDtypeStruct((heads, 1, t), _F32)],
        compiler_params=_cp("parallel", "parallel"),
        name="peer_route",
    )(q, k1, k2)
    tb = _tile(t, 512, 128)
    n_sub = 8
    eb = n_sub * nkeys
    xb = x.astype(_BF16)
    return pl.pallas_call(
        functools.partial(_peer_dense_kernel, heads=heads, nkeys=nkeys),
        grid=(t // tb, nkeys // n_sub),
        in_specs=[pl.BlockSpec((tb, d), lambda i, e: (i, 0)),
                  pl.BlockSpec((eb, d), lambda i, e: (e, 0)),
                  pl.BlockSpec((d, eb), lambda i, e: (0, e)),
                  pl.BlockSpec((heads, n_sub, tb), lambda i, e: (0, e, i)),
                  pl.BlockSpec((heads, nkeys, tb), lambda i, e: (0, 0, i)),
                  pl.BlockSpec((heads, 1, tb), lambda i, e: (0, 0, i))],
        out_specs=pl.BlockSpec((d, tb), lambda i, e: (0, i)),
        out_shape=jax.ShapeDtypeStruct((d, t), _F32),
        scratch_shapes=[pltpu.VMEM((eb, tb), _F32), pltpu.VMEM((eb, tb), _BF16)],
        compiler_params=_cp("parallel", "arbitrary"),
        name="peer_dense",
    )(xb, u_bf, vt_bf, e1, e2, th)


def _hyena_na_mixer(h_lat, h_ctx, mod_l, mod_c, w_in, conv_w, conv_b, fparams, skip, rpb, w_out):
    split = conv_w.shape[1]
    wb = w_in.astype(_BF16)
    w_hy, w_na = wb[:, :split], wb[:, split:]
    outs = []
    qkvs = []
    for hh, (sc, sh) in ((h_lat, mod_l), (h_ctx, mod_c)):
        p_hy = _mm(hh, w_hy, scale=sc, shift=sh, name="hy_proj")
        qkvs.append(_mm(hh, w_na, scale=sc, shift=sh, out_dtype=_BF16, name="na_proj"))
        outs.append(_hyena(p_hy, conv_w, conv_b, fparams, skip))
    na_lat = _natten(qkvs[0], qkvs[1], rpb)
    na_ctx = _ctx_attn(qkvs[1])
    wo = w_out.astype(_BF16)
    y_lat = _mm(jnp.concatenate([outs[0], na_lat], axis=-1), wo, name="ab_out")
    y_ctx = _mm(jnp.concatenate([outs[1], na_ctx], axis=-1), wo, name="ab_out")
    return y_lat, y_ctx


def kernel(x, c, ctx, c_ctx, ada_w, ada_b, ln_w, ln_b, ab_w_in, hy_conv_w, hy_conv_b, hy_fw1, hy_fb1, hy_fw2, hy_fb2, hy_fw3, hy_fb3, hy_fw4, hy_skip, na_rpb, ab_w_out, ml_w_in, ml_b_gates, ml_conv_w, ml_conv_b, ml_norm_w, ml_w_out, peer_wq, peer_k1, peer_k2, peer_u, peer_v):
    bsz, seq_len, d = x.shape
    depth = ada_w.shape[0]
    assert bsz == 1 and depth == 2, "one even (Hyena/attention) layer followed by one last odd (mLSTM) layer"
    alpha = (2 * depth) ** 0.25
    h_lat, h_ctx = x[0], ctx[0]
    cvec = jnp.zeros((8, d), _F32).at[0].set(c[0]).at[1].set(c_ctx)
    row = lambda a: a.reshape(1, d)
    for layer in range(depth):
        last = layer == depth - 1
        mod = _mm(cvec, ada_w[layer], pre="silu", bias=ada_b[layer].reshape(1, 6 * d), passes=3, name="ada_mod")
        sh_l, sc_l, g_l, shf_l, scf_l, gf_l = [mod[0:1, i * d:(i + 1) * d] for i in range(6)]
        sh_c, sc_c, g_c, shf_c, scf_c, gf_c = [mod[1:2, i * d:(i + 1) * d] for i in range(6)]
        if layer % 2 == 0:
            fparams = (hy_fw1[0], hy_fb1[0], hy_fw2[0], hy_fb2[0], hy_fw3[0], hy_fb3[0], hy_fw4[0])
            y_lat, y_ctx = _hyena_na_mixer(h_lat, h_ctx, (sc_l, sh_l), (sc_c, sh_c), ab_w_in[0], hy_conv_w[0],
                                           hy_conv_b[0], fparams, hy_skip[0], na_rpb[0], ab_w_out[0])
        else:
            y_lat = _mlstm_mixer(h_lat, h_ctx, (sc_l, sh_l), (sc_c, sh_c), ml_w_in[0], ml_b_gates[0], ml_conv_w[0],
                                 ml_conv_b[0], ml_norm_w[0], ml_w_out[0])
            y_ctx = None
        lw0, lb0, lw1, lb1 = row(ln_w[layer, 0]), row(ln_b[layer, 0]), row(ln_w[layer, 1]), row(ln_b[layer, 1])
        u_bf = peer_u[layer].astype(_BF16)
        vt_bf = peer_v[layer].astype(_BF16).T
        peer = lambda v: _peer(v, peer_wq[layer], peer_k1[layer], peer_k2[layer], u_bf, vt_bf)
        h_lat, v_lat = _ln_res(h_lat, y_lat, g_l, lw0, lb0, scf_l, shf_l, alpha=alpha, with_mod=True)
        h_lat, _ = _ln_res(h_lat, peer(v_lat), gf_l, lw1, lb1, scf_l, shf_l, alpha=alpha, with_mod=False,
                           y_transposed=True)
        if not last:
            h_ctx, v_ctx = _ln_res(h_ctx, y_ctx, g_c, lw0, lb0, scf_c, shf_c, alpha=alpha, with_mod=True)
            h_ctx, _ = _ln_res(h_ctx, peer(v_ctx), gf_c, lw1, lb1, scf_c, shf_c, alpha=alpha, with_mod=False,
                               y_transposed=True)
    return h_lat[None]
```

```python
import functools
import math

import numpy as np
import jax
import jax.numpy as jnp
from jax import lax
from jax.experimental import pallas as pl
from jax.experimental.pallas import tpu as pltpu

_F32 = jnp.float32
_BF16 = jnp.bfloat16

GRID_W = 64
HY_ORDER = 2
HY_EMB = 33
HY_DECAY_TARGET = 1e-2
HY_FAST_PCT = 0.3
HY_SLOW_PCT = 1.5
NA_HEADS = 8
NA_WIN_ROWS = 8
NA_WIN_COLS = 16
ML_HEADS = 8
ML_TILE = 256
ML_M0 = -1e30
ROPE_BASE = 10000.0
PEER_HEADS = 8
PEER_NKEYS = 128
PEER_TOPK = 16
LN_EPS = 1e-6

_VMEM_LIMIT = 52 * 1024 * 1024
_NEG = -1e30
_FFT_DT = jnp.bfloat16

_NT = (((1,), (1,)), ((), ()))
_TN = (((0,), (0,)), ((), ()))


def _cp(*sem):
    return pltpu.CompilerParams(dimension_semantics=sem, vmem_limit_bytes=_VMEM_LIMIT)


def _tile(n, cap, align):
    t = (min(cap, n) // align) * align
    while t >= align:
        if n % t == 0:
            return t
        t -= align
    return n


def _split(x):
    hi = x.astype(_BF16)
    lo = (x.astype(_F32) - hi.astype(_F32)).astype(_BF16)
    return hi, lo


def _dot(a, b, passes=1, dims=None):
    dn = dims or (((a.ndim - 1,), (0,)), ((), ()))
    f = lambda x, y: lax.dot_general(x, y, dn, preferred_element_type=_F32)
    if passes == 1:
        return f(a.astype(_BF16), b.astype(_BF16))
    ah, al = _split(a)
    bh, bl = _split(b)
    return f(ah, bh) + (f(ah, bl) + f(al, bh))


def _sigmoid(x):
    return 1.0 / (1.0 + jnp.exp(-x))


def _mm_kernel(*refs, pre, has_bias, passes):
    it = iter(refs)
    a_ref = next(it)
    if pre == "mod":
        sc_ref, sh_ref = next(it), next(it)
    b_ref = next(it)
    bias_ref = next(it) if has_bias else None
    o_ref = next(it)
    a = a_ref[...]
    if pre == "mod":
        a = a.astype(_F32) * (1.0 + sc_ref[...]) + sh_ref[...]
    elif pre == "silu":
        a = a * _sigmoid(a)
    acc = _dot(a, b_ref[...], passes)
    if has_bias:
        acc = acc + bias_ref[...]
    o_ref[...] = acc.astype(o_ref.dtype)


def _mm(a, b, *, scale=None, shift=None, pre=None, bias=None, out_dtype=_F32, passes=1, name="mm"):
    m, k = a.shape
    n = b.shape[1]
    if scale is not None:
        pre = "mod"
    tile_bytes = 4 * 1024 * 1024
    tm = _tile(m, max(8, min(1024, 2 * tile_bytes // (k * a.dtype.itemsize))), 8)
    tn = _tile(n, max(128, min(8192, tile_bytes // (k * b.dtype.itemsize), tile_bytes // (tm * 4))), 128)
    in_specs = [pl.BlockSpec((tm, k), lambda i, j: (i, 0))]
    args = [a]
    if pre == "mod":
        in_specs += [pl.BlockSpec((1, k), lambda i, j: (0, 0))] * 2
        args += [scale, shift]
    in_specs.append(pl.BlockSpec((k, tn), lambda i, j: (0, j)))
    args.append(b)
    if bias is not None:
        in_specs.append(pl.BlockSpec((1, tn), lambda i, j: (0, j)))
        args.append(bias)
    return pl.pallas_call(
        functools.partial(_mm_kernel, pre=pre, has_bias=bias is not None, passes=passes),
        grid=(m // tm, n // tn),
        in_specs=in_specs,
        out_specs=pl.BlockSpec((tm, tn), lambda i, j: (i, j)),
        out_shape=jax.ShapeDtypeStruct((m, n), out_dtype),
        compiler_params=_cp("parallel", "parallel"),
        name=name,
    )(*args)


def _ln_res_kernel(h_ref, y_ref, g_ref, w_ref, b_ref, sc_ref, sh_ref, *o_refs, alpha, with_mod, y_transposed):
    y = y_ref[...].T if y_transposed else y_ref[...]
    z = alpha * h_ref[...] + g_ref[...] * y
    mu = jnp.mean(z, axis=-1, keepdims=True)
    zc = z - mu
    var = jnp.mean(zc * zc, axis=-1, keepdims=True)
    hn = zc * lax.rsqrt(var + LN_EPS) * w_ref[...] + b_ref[...]
    o_refs[0][...] = hn
    if with_mod:
        o_refs[1][...] = hn * (1.0 + sc_ref[...]) + sh_ref[...]


def _ln_res(h, y, g, w, b, sc, sh, *, alpha, with_mod, y_transposed=False):
    m, d = h.shape
    tr = _tile(m, 256, 128 if y_transposed else 8)
    row = pl.BlockSpec((tr, d), lambda i: (i, 0))
    y_spec = pl.BlockSpec((d, tr), lambda i: (0, i)) if y_transposed else row
    vec = pl.BlockSpec((1, d), lambda i: (0, 0))
    n_out = 2 if with_mod else 1
    out = pl.pallas_call(
        functools.partial(_ln_res_kernel, alpha=alpha, with_mod=with_mod, y_transposed=y_transposed),
        grid=(m // tr,),
        in_specs=[row, y_spec, vec, vec, vec, vec, vec],
        out_specs=[row] * n_out,
        out_shape=[jax.ShapeDtypeStruct((m, d), _F32)] * n_out,
        compiler_params=_cp("parallel"),
        name="ln_res",
    )(h, y, g, w, b, sc, sh)
    return out if with_mod else (out[0], None)


def _conv3_kernel(*refs, n_out, act, rope, out_scales):
    it = iter(refs)
    xp_ref, x_ref, xn_ref, w_ref, b_ref = (next(it) for _ in range(5))
    if rope:
        cos_ref, sin_ref = next(it), next(it)
    o_refs = [next(it) for _ in range(n_out)]
    i = pl.program_id(0)
    x = x_ref[...]
    tr = x.shape[0]
    rows = lax.broadcasted_iota(jnp.int32, x.shape, 0)
    prev_row = jnp.where(i > 0, xp_ref[7:8, :], 0.0)
    next_row = jnp.where(i < pl.num_programs(0) - 1, xn_ref[0:1, :], 0.0)
    x_m1 = jnp.where(rows == 0, prev_row, pltpu.roll(x, 1, axis=0))
    x_p1 = jnp.where(rows == tr - 1, next_row, pltpu.roll(x, tr - 1, axis=0))
    y = b_ref[...] + x_m1 * w_ref[0:1, :] + x * w_ref[1:2, :] + x_p1 * w_ref[2:3, :]
    if act:
        y = y * _sigmoid(y)
    wo = y.shape[1] // n_out
    for o in range(n_out):
        yo = y[:, o * wo:(o + 1) * wo]
        if rope:
            cos, sin = cos_ref[...], sin_ref[...]
            lane = lax.broadcasted_iota(jnp.int32, cos.shape, 1)
            first = (lane % 64) < 32
            parts = []
            for hh in range(wo // 128):
                yh = yo[:, hh * 128:(hh + 1) * 128]
                sw = jnp.where(first, pltpu.roll(yh, 96, axis=1), pltpu.roll(yh, 32, axis=1))
                parts.append(yh * cos + sw * sin)
            yo = jnp.concatenate(parts, axis=1)
        if out_scales[o] != 1.0:
            yo = yo * out_scales[o]
        o_refs[o][...] = yo.astype(o_refs[o].dtype)


def _conv3(x, w, b, *, n_out, out_dtype, act=False, rope=None, out_scales=None):
    l, cx = x.shape
    tr = _tile(l, 256, 8)
    nb8 = l // 8
    out_scales = out_scales or (1.0,) * n_out
    in_specs = [
        pl.BlockSpec((8, cx), lambda i: (jnp.maximum(i * (tr // 8) - 1, 0), 0)),
        pl.BlockSpec((tr, cx), lambda i: (i, 0)),
        pl.BlockSpec((8, cx), lambda i: (jnp.minimum((i + 1) * (tr // 8), nb8 - 1), 0)),
        pl.BlockSpec((3, cx), lambda i: (0, 0)),
        pl.BlockSpec((1, cx), lambda i: (0, 0)),
    ]
    args = [x, x, x, w, b.reshape(1, cx)]
    if rope is not None:
        in_specs += [pl.BlockSpec((tr, 128), lambda i: (i, 0))] * 2
        args += list(rope)
    wo = cx // n_out
    return pl.pallas_call(
        functools.partial(_conv3_kernel, n_out=n_out, act=act, rope=rope is not None, out_scales=out_scales),
        grid=(l // tr,),
        in_specs=in_specs,
        out_specs=[pl.BlockSpec((tr, wo), lambda i: (i, 0))] * n_out,
        out_shape=[jax.ShapeDtypeStruct((l, wo), out_dtype)] * n_out,
        compiler_params=_cp("parallel"),
        name="conv3",
    )(*args)


def _hy_filter_kernel(f_ref, t_ref, w1_ref, b1_ref, w2_ref, b2_ref, w3_ref, b3_ref, w4_ref, d_ref, o_ref):
    h = jnp.sin(_dot(f_ref[...], w1_ref[...], 3) + b1_ref[...])
    h = jnp.sin(_dot(h, w2_ref[...], 3) + b2_ref[...])
    h = jnp.sin(_dot(h, w3_ref[...], 3) + b3_ref[...])
    taps = _dot(h, w4_ref[...], 3)
    tv = t_ref[...]
    window = jnp.exp(-tv[:, 0:1] * d_ref[...]) * tv[:, 1:2]
    c = window.shape[1]
    for g in range(taps.shape[1] // c):
        o_ref[:, g * c:(g + 1) * c] = (taps[:, g * c:(g + 1) * c] * window).astype(o_ref.dtype)


def _hy_taps(seq_len, l_pad, fw1, fb1, fw2, fb2, fw3, fb3, fw4, width):
    hid = fw1.shape[1]
    n = np.arange(2 * l_pad)
    off = np.where(n < l_pad, n, 2 * l_pad - n)
    exists = np.where(n < l_pad, off < seq_len, (off >= 1) & (off < seq_len))
    pos = jnp.asarray(np.where(exists, off, 0), _F32)
    t = pos / (seq_len - 1)
    omega = 2.0 * math.pi * pos / seq_len
    t = jnp.stack([t, jnp.asarray(exists, _F32)], axis=-1)
    bands = (HY_EMB - 1) // 2
    freqs = jnp.linspace(1e-4, bands - 1, bands, dtype=_F32)
    ang = omega[:, None] * freqs[None, :]
    kpad = 64
    feats = jnp.concatenate([t[:, 0:1], jnp.cos(ang), -jnp.sin(ang), jnp.zeros((2 * l_pad, kpad - HY_EMB), _F32)],
                            axis=-1)
    w1 = jnp.concatenate([fw1, jnp.zeros((kpad - HY_EMB, hid), _F32)], axis=0)
    max_decay = math.log(HY_DECAY_TARGET) / HY_FAST_PCT
    min_decay = math.log(HY_DECAY_TARGET) / HY_SLOW_PCT
    deltas = jnp.abs(jnp.linspace(min_decay, max_decay, width, dtype=_F32))[None, :]
    n_cols = fw4.shape[1] // 2
    tl = _tile(l_pad, 256, 8)
    per_dir = l_pad // tl
    full = lambda shape: pl.BlockSpec(shape, lambda i: (0,) * len(shape))
    return pl.pallas_call(
        _hy_filter_kernel,
        grid=(2 * per_dir,),
        in_specs=[pl.BlockSpec((tl, kpad), lambda i: (i, 0)), pl.BlockSpec((tl, 2), lambda i: (i, 0)),
                  full((kpad, hid)), full((1, hid)), full((hid, hid)), full((1, hid)), full((hid, hid)), full((1, hid)),
                  pl.BlockSpec((hid, n_cols), lambda i: (0, i // per_dir)), full((1, width))],
        out_specs=pl.BlockSpec((tl, n_cols), lambda i: (i, 0)),
        out_shape=jax.ShapeDtypeStruct((2 * l_pad, n_cols), _BF16),
        compiler_params=_cp("parallel"),
        name="hy_filter",
    )(feats, t, w1, fb1.reshape(1, hid), fw2, fb2.reshape(1, hid), fw3, fb3.reshape(1, hid), fw4, deltas)


def _fft_consts(n1, n2):
    n = n1 * n2
    k1 = np.arange(n1)[:, None]
    j1 = np.arange(n1 // 2)[None, :]
    ang1 = 2.0 * np.pi * k1 * j1 / n1
    f1_fwd = np.concatenate([np.cos(ang1), -np.sin(ang1)], axis=0)
    f1_inv = np.concatenate([np.cos(ang1).T, -np.sin(ang1).T], axis=1) / n
    k2 = np.arange(n2)[:, None]
    j2 = np.arange(n2)[None, :]
    ang2 = 2.0 * np.pi * k2 * j2 / n2
    f2 = np.stack([np.cos(ang2), -np.sin(ang2)])
    angt = 2.0 * np.pi * np.arange(n1)[:, None] * np.arange(n2)[None, :] / n
    tw = np.stack([np.cos(angt), -np.sin(angt)], axis=1)
    angf = 2.0 * np.pi * k1 * np.arange(n1)[None, :] / n1
    f1_full = np.concatenate([np.cos(angf), -np.sin(angf)], axis=0)
    as32 = lambda a: jnp.asarray(a, dtype=_F32)
    return as32(f1_fwd), as32(f1_inv), as32(f2), as32(tw), as32(np.transpose(tw, (0, 2, 1))), as32(f1_full)


def _cmm(f_stack, xr, xi):
    n = f_stack.shape[0] // 2
    p1 = jnp.dot(f_stack, xr.astype(_BF16), preferred_element_type=_F32)
    p2 = jnp.dot(f_stack, xi.astype(_BF16), preferred_element_type=_F32)
    return p1[:n] - p2[n:], p2[:n] + p1[n:]


def _twiddled_dft(f2_ref, tw_ref):
    f2r, f2i = f2_ref[0], f2_ref[1]
    tr, ti = tw_ref[0, 0:1, :], tw_ref[0, 1:2, :]
    return jnp.concatenate([f2r * tr - f2i * ti, f2r * ti + f2i * tr], axis=0).astype(_BF16)


def _fft_spec_kernel(a_ref, f2_ref, tw_ref, o_ref):
    fw = _twiddled_dft(f2_ref, tw_ref)
    hr, hi = _cmm(fw, a_ref[0, 0], a_ref[1, 0])
    o_ref[0, 0, 0] = hr.astype(o_ref.dtype)
    o_ref[0, 1, 0] = hi.astype(o_ref.dtype)


def _fft_mid_kernel(a_ref, h_ref, f2_ref, tw_ref, twc_ref, o_ref):
    fw = _twiddled_dft(f2_ref, tw_ref)
    br, bi = _cmm(fw, a_ref[0, 0], a_ref[1, 0])
    hr, hi = h_ref[0, 0, 0].astype(_F32), h_ref[0, 1, 0].astype(_F32)
    yr = (br * hr - bi * hi).astype(_BF16)
    yi = (br * hi + bi * hr).astype(_BF16)
    f2r, f2i = f2_ref[0], f2_ref[1]
    tcr, tci = twc_ref[0, :, 0:1], twc_ref[0, :, 1:2]
    g = jnp.concatenate([tcr * f2r - tci * f2i, -(tcr * f2i + tci * f2r)], axis=0).astype(_BF16)
    zr, zi = _cmm(g, yr, yi)
    o_ref[0, 0] = zr.astype(o_ref.dtype)
    o_ref[1, 0] = zi.astype(o_ref.dtype)


def _fft_shape(l_pad):
    n2 = 256 if l_pad >= 8192 else 64
    n1 = 2 * l_pad // n2
    return n1, n2


def _hy_spectra(taps, width, consts, n1, n2):
    f1_fwd, _, f2, tw, _, f1_full = consts
    half = taps.shape[1]
    a = _mm(f1_full, taps.reshape(n1, n2 * half), out_dtype=_FFT_DT, name="fft_a_filt")
    a = a.reshape(2, n1, n2, half)
    return pl.pallas_call(
        _fft_spec_kernel,
        grid=(n1, HY_ORDER),
        in_specs=[pl.BlockSpec((2, 1, n2, width), lambda k, o: (0, k, 0, o)),
                  pl.BlockSpec((2, n2, n2), lambda k, o: (0, 0, 0)),
                  pl.BlockSpec((1, 2, n2), lambda k, o: (k, 0, 0))],
        out_specs=pl.BlockSpec((1, 2, 1, n2, width), lambda k, o: (o, 0, k, 0, 0)),
        out_shape=jax.ShapeDtypeStruct((HY_ORDER, 2, n1, n2, width), _FFT_DT),
        compiler_params=_cp("parallel", "parallel"),
        name="fft_spec",
    )(a, f2, tw)


def _long_conv(z, spec, order, consts, n1, n2):
    f1_fwd, f1_inv, f2, tw, twc, _ = consts
    l_pad, c = z.shape
    a = _mm(f1_fwd, z.reshape(n1 // 2, n2 * c), out_dtype=_FFT_DT, name="fft_a")
    a = a.reshape(2, n1, n2, c)
    blk = pl.BlockSpec((2, 1, n2, c), lambda k: (0, k, 0, 0))
    mid = pl.pallas_call(
        _fft_mid_kernel,
        grid=(n1,),
        in_specs=[blk, pl.BlockSpec((1, 2, 1, n2, c), lambda k: (order, 0, k, 0, 0)),
                  pl.BlockSpec((2, n2, n2), lambda k: (0, 0, 0)),
                  pl.BlockSpec((1, 2, n2), lambda k: (k, 0, 0)), pl.BlockSpec((1, n2, 2), lambda k: (k, 0, 0))],
        out_specs=blk,
        out_shape=jax.ShapeDtypeStruct((2, n1, n2, c), _FFT_DT),
        compiler_params=_cp("parallel"),
        name="fft_mid",
    )(a, spec, f2, tw, twc)
    y = _mm(f1_inv, mid.reshape(2 * n1, n2 * c), name="fft_c")
    return y.reshape(l_pad, c)


def _hy_gate_kernel(y_ref, z_ref, s_ref, g_ref, o_ref):
    o_ref[...] = (g_ref[...] * (y_ref[...] + z_ref[...] * s_ref[...])).astype(o_ref.dtype)


def _hy_gate(y, z, skip, gate, out_dtype):
    l, c = y.shape
    tr = _tile(l, 512, 8)
    row = pl.BlockSpec((tr, c), lambda i: (i, 0))
    return pl.pallas_call(
        _hy_gate_kernel,
        grid=(l // tr,),
        in_specs=[row, row, pl.BlockSpec((1, c), lambda i: (0, 0)), row],
        out_specs=row,
        out_shape=jax.ShapeDtypeStruct((l, c), out_dtype),
        compiler_params=_cp("parallel"),
        name="hy_gate",
    )(y, z, skip.reshape(1, c), gate)


def _hyena(p, conv_w, conv_b, fparams, skip):
    l = p.shape[0]
    width = p.shape[1] // (HY_ORDER + 1)
    l_pad = max(l, 1024)
    n1, n2 = _fft_shape(l_pad)
    consts = _fft_consts(n1, n2)
    taps = _hy_taps(l, l_pad, *fparams, width)
    parts = _conv3(p, conv_w, conv_b, n_out=HY_ORDER + 1, out_dtype=_F32)
    if l_pad != l:
        parts = [jnp.pad(a, ((0, l_pad - l), (0, 0))) for a in parts]
    spec = _hy_spectra(taps, width, consts, n1, n2)
    z = parts[0]
    for o in range(HY_ORDER):
        y = _long_conv(z, spec, o, consts, n1, n2)
        z = _hy_gate(y, z, skip[o], parts[o + 1], _F32 if o + 1 < HY_ORDER else _BF16)
    return z[:l]


def _na_bias_table(rpb, w):
    kr, kc = NA_WIN_ROWS, NA_WIN_COLS
    cols = np.arange(w)
    start = np.clip(cols - kc // 2, 0, w - kc)
    kcol = np.arange(w)[None, :]
    in_win = (kcol >= start[:, None]) & (kcol < start[:, None] + kc)
    rp = jnp.pad(rpb, ((0, 0), (0, 0), (w - 1, w)))
    by_q = jnp.stack([rp[:, :, kc - 1 + w - 1 - q:kc - 1 + 2 * w - 1 - q] for q in range(w)], axis=2)
    by_q = jnp.where(in_win[None, None], by_q, _NEG)
    t = jnp.stack([jnp.concatenate([by_q[:, i - d + kr - 1] for i in range(kr)], axis=-1) for d in range(kr)], axis=1)
    return t


def _natten_kernel(q_ref, k_ref, v_ref, kc_ref, vc_ref, bias_ref, o_ref, *, rows, w, scale):
    qb = pl.program_id(1)
    kr = NA_WIN_ROWS
    kc, vc = kc_ref[...], vc_ref[...]
    for a in range(kr):
        r = qb * kr + a
        r0 = jnp.clip(r - kr // 2, 0, rows - kr)
        d = r - r0
        start = pl.multiple_of(r0 * w, w)
        kwin = k_ref[pl.ds(start, kr * w), :]
        vwin = v_ref[pl.ds(start, kr * w), :]
        qa = q_ref[a * w:(a + 1) * w, :]
        s_lat = _dot(qa, kwin, dims=_NT) * scale + bias_ref[0, d]
        s_ctx = _dot(qa, kc, dims=_NT) * scale
        m = jnp.maximum(jnp.max(s_lat, axis=-1, keepdims=True), jnp.max(s_ctx, axis=-1, keepdims=True))
        p_lat = jnp.exp(s_lat - m)
        p_ctx = jnp.exp(s_ctx - m)
        den = jnp.sum(p_lat, axis=-1, keepdims=True) + jnp.sum(p_ctx, axis=-1, keepdims=True)
        o = _dot(p_lat, vwin) + _dot(p_ctx, vc)
        o_ref[a * w:(a + 1) * w, :] = (o / den).astype(o_ref.dtype)


def _natten(qkv, kv_ctx, rpb):
    l = qkv.shape[0]
    h = NA_HEADS
    hd = qkv.shape[1] // (3 * h)
    w = GRID_W
    rows = l // w
    kr = NA_WIN_ROWS
    assert rows % kr == 0 and rows >= kr and hd % 128 == 0
    lc = kv_ctx.shape[0]
    bias = _na_bias_table(rpb, w)
    tq = kr * w
    return pl.pallas_call(
        functools.partial(_natten_kernel, rows=rows, w=w, scale=hd ** -0.5),
        grid=(h, rows // kr),
        in_specs=[pl.BlockSpec((tq, hd), lambda hh, qb: (qb, hh)),
                  pl.BlockSpec((l, hd), lambda hh, qb: (0, h + hh)),
                  pl.BlockSpec((l, hd), lambda hh, qb: (0, 2 * h + hh)),
                  pl.BlockSpec((lc, hd), lambda hh, qb: (0, h + hh)),
                  pl.BlockSpec((lc, hd), lambda hh, qb: (0, 2 * h + hh)),
                  pl.BlockSpec((1, kr, w, kr * w), lambda hh, qb: (hh, 0, 0, 0))],
        out_specs=pl.BlockSpec((tq, hd), lambda hh, qb: (qb, hh)),
        out_shape=jax.ShapeDtypeStruct((l, h * hd), _BF16),
        compiler_params=_cp("parallel", "parallel"),
        name="natten",
    )(qkv, qkv, qkv, kv_ctx, kv_ctx, bias)


def _ctx_attn_kernel(q_ref, k_ref, v_ref, o_ref, *, scale):
    s = _dot(q_ref[...], k_ref[...], dims=_NT) * scale
    p = jnp.exp(s - jnp.max(s, axis=-1, keepdims=True))
    den = jnp.sum(p, axis=-1, keepdims=True)
    o_ref[...] = (_dot(p, v_ref[...]) / den).astype(o_ref.dtype)


def _ctx_attn(qkv):
    lc = qkv.shape[0]
    h = NA_HEADS
    hd = qkv.shape[1] // (3 * h)
    blk = lambda off: pl.BlockSpec((lc, hd), lambda hh: (0, off + hh))
    return pl.pallas_call(
        functools.partial(_ctx_attn_kernel, scale=hd ** -0.5),
        grid=(h,),
        in_specs=[blk(0), blk(h), blk(2 * h)],
        out_specs=blk(0),
        out_shape=jax.ShapeDtypeStruct((lc, h * hd), _BF16),
        compiler_params=_cp("parallel"),
        name="ctx_attn",
    )(qkv, qkv, qkv)


def _ml_gates_kernel(g_ref, b_ref, o_ref, *, heads):
    g = g_ref[...] + b_ref[...]
    logsig = jnp.minimum(g, 0.0) - jnp.log(1.0 + jnp.exp(-jnp.abs(g)))
    lane = lax.broadcasted_iota(jnp.int32, g.shape, 1)
    is_forget = (lane // heads) % 2 == 1
    o_ref[...] = jnp.where(is_forget, logsig, g)


def _ml_gates(g_raw, b_gates):
    l, n = g_raw.shape
    tr = _tile(l, 1024, 8)
    bias = jnp.pad(b_gates, (0, n - b_gates.shape[0])).reshape(1, n)
    return pl.pallas_call(
        functools.partial(_ml_gates_kernel, heads=ML_HEADS),
        grid=(l // tr,),
        in_specs=[pl.BlockSpec((tr, n), lambda i: (i, 0)), pl.BlockSpec((1, n), lambda i: (0, 0))],
        out_specs=pl.BlockSpec((tr, n), lambda i: (i, 0)),
        out_shape=jax.ShapeDtypeStruct((l, n), _F32),
        compiler_params=_cp("parallel"),
        name="ml_gates",
    )(g_raw, bias)


def _ml_scan_kernel(*refs, heads, dk, dv, reverse, with_h):
    it = iter(refs)
    if with_h:
        q_ref = next(it)
    k_ref, v_ref, g_ref, gt_ref, c0_ref, n0_ref, m0_ref = (next(it) for _ in range(7))
    if with_h:
        h_ref = next(it)
    c_ref, n_ref, m_ref = next(it), next(it), next(it)

    @pl.when(pl.program_id(0) == 0)
    def _():
        c_ref[...] = c0_ref[...]
        n_ref[...] = n0_ref[...]
        m_ref[...] = m0_ref[...]

    t = k_ref.shape[0]
    ti = lax.broadcasted_iota(jnp.int32, (t, t), 0)
    si = lax.broadcasted_iota(jnp.int32, (t, t), 1)
    seen = (si >= ti) if reverse else (si <= ti)
    seen_t = (ti >= si) if reverse else (ti <= si)
    g = g_ref[...]
    gt = gt_ref[0]
    c0 = 2 * heads if reverse else 0
    for h in range(heads):
        i_col = g[:, c0 + h:c0 + h + 1]
        f_col = g[:, c0 + heads + h:c0 + heads + h + 1]
        i_row = gt[c0 + h:c0 + h + 1, :]
        f_row = gt[c0 + heads + h:c0 + heads + h + 1, :]
        b_col = jnp.sum(jnp.where(seen, f_row, 0.0), axis=1, keepdims=True)
        b_row = jnp.sum(jnp.where(seen_t, f_col, 0.0), axis=0, keepdims=True)
        b_end = jnp.sum(f_row, axis=1, keepdims=True)
        m_prev = m_ref[h][:, 0:1]
        w_log_row = b_end - b_row + i_row
        w_log_col = b_end - b_col + i_col
        m_new = jnp.maximum(b_end + m_prev, jnp.max(w_log_row, axis=1, keepdims=True))
        decay = jnp.exp(b_end + m_prev - m_new)
        w_col = jnp.exp(w_log_col - m_new)
        kh = k_ref[:, h * dk:(h + 1) * dk]
        vh = v_ref[:, h * dv:(h + 1) * dv]
        ct = c_ref[h]
        nv = n_ref[h]
        if with_h:
            qh = q_ref[:, h * dk:(h + 1) * dk]
            log_d = jnp.where(seen, b_col - b_row + i_row, -jnp.inf)
            m_in = b_col + m_prev
            m_t = jnp.maximum(m_in, jnp.max(log_d, axis=1, keepdims=True))
            sc = _dot(qh, kh, dims=_NT) * jnp.exp(log_d - m_t)
            g_in = jnp.exp(m_in - m_t)
            num = _dot(sc, vh) + _dot(qh, ct) * g_in
            den = jnp.sum(sc, axis=1, keepdims=True) + jnp.sum(qh.astype(_F32) * nv, axis=1, keepdims=True) * g_in
            h_ref[:, h * dv:(h + 1) * dv] = num / jnp.maximum(jnp.abs(den), jnp.exp(-m_t))
        vw = vh.astype(_F32) * w_col
        c_ref[h] = decay * ct + _dot(kh, vw, dims=_TN)
        n_ref[h] = decay * nv + jnp.sum(kh.astype(_F32) * w_col, axis=0, keepdims=True)
        m_ref[h] = jnp.broadcast_to(m_new, m_ref.shape[1:])


def _ml_scan(q, k, v, gates, state, *, reverse, dk, dv):
    l = k.shape[0]
    heads = ML_HEADS
    t = min(ML_TILE, l)
    assert l % t == 0
    nc = l // t
    with_h = q is not None
    gt = jnp.transpose(gates.reshape(nc, t, gates.shape[1]), (0, 2, 1))
    cidx = (lambda j: nc - 1 - j) if reverse else (lambda j: j)
    row = lambda width: pl.BlockSpec((t, width), lambda j: (cidx(j), 0))
    st_specs = [pl.BlockSpec((heads, dk, dv), lambda j: (0, 0, 0)), pl.BlockSpec((heads, 1, dk), lambda j: (0, 0, 0)),
                pl.BlockSpec((heads, 1, 128), lambda j: (0, 0, 0))]
    st_shapes = [jax.ShapeDtypeStruct((heads, dk, dv), _F32), jax.ShapeDtypeStruct((heads, 1, dk), _F32),
                 jax.ShapeDtypeStruct((heads, 1, 128), _F32)]
    in_specs = ([row(heads * dk)] if with_h else []) + [
        row(heads * dk), row(heads * dv), row(gates.shape[1]),
        pl.BlockSpec((1, gates.shape[1], t), lambda j: (cidx(j), 0, 0))] + st_specs
    args = ([q] if with_h else []) + [k, v, gates, gt, *state]
    out_specs = ([row(heads * dv)] if with_h else []) + st_specs
    out_shape = ([jax.ShapeDtypeStruct((l, heads * dv), _F32)] if with_h else []) + st_shapes
    out = pl.pallas_call(
        functools.partial(_ml_scan_kernel, heads=heads, dk=dk, dv=dv, reverse=reverse, with_h=with_h),
        grid=(nc,),
        in_specs=in_specs,
        out_specs=out_specs,
        out_shape=out_shape,
        compiler_params=_cp("arbitrary"),
        name="ml_scan",
    )(*args)
    return (out[0], tuple(out[1:])) if with_h else (None, tuple(out))


def _ml_merge_kernel(hf_ref, hb_ref, o_ref, w_ref, out_ref, *, heads, dv):
    for h in range(heads):
        sl = slice(h * dv, (h + 1) * dv)
        x = hf_ref[:, sl] + hb_ref[:, sl]
        mu = jnp.mean(x, axis=-1, keepdims=True)
        xc = x - mu
        var = jnp.mean(xc * xc, axis=-1, keepdims=True)
        hn = xc * lax.rsqrt(var + LN_EPS) * w_ref[:, sl]
        out_ref[:, sl] = (hn * _sigmoid(o_ref[:, sl])).astype(out_ref.dtype)


def _ml_merge(hf, hb, o, norm_w, dv):
    l, n = hf.shape
    tr = _tile(l, 256, 8)
    row = pl.BlockSpec((tr, n), lambda i: (i, 0))
    return pl.pallas_call(
        functools.partial(_ml_merge_kernel, heads=ML_HEADS, dv=dv),
        grid=(l // tr,),
        in_specs=[row, row, row, pl.BlockSpec((1, n), lambda i: (0, 0))],
        out_specs=row,
        out_shape=jax.ShapeDtypeStruct((l, n), _BF16),
        compiler_params=_cp("parallel"),
        name="ml_merge",
    )(hf, hb, o, norm_w.reshape(1, n))


def _rope_tables(l):
    quarter = 32
    pos = jnp.arange(l)
    inv = ROPE_BASE ** (-jnp.arange(quarter, dtype=_F32) / quarter)
    ang_r = (pos // GRID_W).astype(_F32)[:, None] * inv[None]
    ang_c = (pos % GRID_W).astype(_F32)[:, None] * inv[None]
    cos = jnp.concatenate([jnp.cos(ang_r)] * 2 + [jnp.cos(ang_c)] * 2, axis=-1)
    sin = jnp.concatenate([-jnp.sin(ang_r), jnp.sin(ang_r), -jnp.sin(ang_c), jnp.sin(ang_c)], axis=-1)
    return cos, sin


def _mlstm_mixer(h_lat, h_ctx, mod_l, mod_c, w_in, b_gates, conv_w, conv_b, norm_w, w_out):
    heads = ML_HEADS
    v_w = w_out.shape[0]
    dv = v_w // heads
    dk = dv // 2
    qk_w = heads * dk
    g0 = 2 * qk_w + 2 * v_w
    n_g = w_in.shape[1] - g0
    assert dk == 128
    l = h_lat.shape[0]
    wb = w_in.astype(_BF16)
    w_g = jnp.pad(w_in[:, g0:], ((0, 0), (0, 128 - n_g)))
    sc_l, sh_l = mod_l
    sc_c, sh_c = mod_c
    qk_l = _mm(h_lat, wb[:, :2 * qk_w], scale=sc_l, shift=sh_l, name="ml_proj_qk")
    v_l = _mm(h_lat, wb[:, 2 * qk_w:2 * qk_w + v_w], scale=sc_l, shift=sh_l, out_dtype=_BF16, name="ml_proj_v")
    o_l = _mm(h_lat, wb[:, 2 * qk_w + v_w:g0], scale=sc_l, shift=sh_l, name="ml_proj_o")
    g_l = _ml_gates(_mm(h_lat, w_g, scale=sc_l, shift=sh_l, passes=3, name="ml_proj_g"), b_gates)
    k_c_raw = _mm(h_ctx, wb[:, qk_w:2 * qk_w], scale=sc_c, shift=sh_c, name="ml_proj_kc")
    v_c = _mm(h_ctx, wb[:, 2 * qk_w:2 * qk_w + v_w], scale=sc_c, shift=sh_c, out_dtype=_BF16, name="ml_proj_vc")
    g_c = _ml_gates(_mm(h_ctx, w_g, scale=sc_c, shift=sh_c, passes=3, name="ml_proj_gc"), b_gates)
    k_scale = dk ** -0.5
    q_l, k_l = _conv3(qk_l, conv_w, conv_b, n_out=2, out_dtype=_BF16, act=True, rope=_rope_tables(l),
                      out_scales=(1.0, k_scale))
    (k_c,) = _conv3(k_c_raw, conv_w[:, qk_w:], conv_b[qk_w:], n_out=1, out_dtype=_BF16, act=True,
                    out_scales=(k_scale,))
    init = (jnp.zeros((heads, dk, dv), _F32), jnp.zeros((heads, 1, dk), _F32), jnp.full((heads, 1, 128), ML_M0, _F32))
    scan = functools.partial(_ml_scan, dk=dk, dv=dv)
    _, st_f = scan(None, k_c, v_c, g_c, init, reverse=False)
    _, st_b = scan(None, k_c, v_c, g_c, init, reverse=True)
    hl_f, _ = scan(q_l, k_l, v_l, g_l, st_f, reverse=False)
    hl_b, _ = scan(q_l, k_l, v_l, g_l, st_b, reverse=True)
    merged = _ml_merge(hl_f, hl_b, o_l, norm_w, dv)
    return _mm(merged, w_out.astype(_BF16), name="ml_out")


_SUBLANES = 8


def _sort_network(n):
    pairs = []
    p = 1
    while p < n:
        k = p
        while k >= 1:
            for j in range(k % p, n - k, 2 * k):
                for i in range(min(k, n - j - k)):
                    if (i + j) // (2 * p) == (i + j + k) // (2 * p):
                        pairs.append((i + j, i + j + k))
            k //= 2
        p *= 2
    return pairs


def _top_desc(x, count):
    n = x.shape[0] // _SUBLANES
    tiles = [x[j * _SUBLANES:(j + 1) * _SUBLANES, :] for j in range(n)]
    for a, b in _sort_network(n):
        tiles[a], tiles[b] = jnp.maximum(tiles[a], tiles[b]), jnp.minimum(tiles[a], tiles[b])
    slot = lax.broadcasted_iota(jnp.int32, tiles[0].shape, 0).astype(_F32)
    vals = []
    for it in range(count):
        m = jnp.max(tiles[0], axis=0, keepdims=True)
        vals.append(m)
        remaining = count - it - 1
        if remaining == 0:
            break
        first = jnp.min(jnp.where(tiles[0] == m, slot, float(_SUBLANES)), axis=0, keepdims=True)
        hit = slot == first
        for j in range(min(remaining, n)):
            below = tiles[j + 1] if j + 1 < n else -jnp.inf
            tiles[j] = jnp.where(hit, below, tiles[j])
    return vals


def _peer_route_kernel(q_ref, k1_ref, k2_ref, e1_ref, e2_ref, th_ref):
    half = k1_ref.shape[1]
    q = q_ref[...]
    s1 = _dot(k1_ref[...], q[:, :half], 3, dims=_NT)
    s2 = _dot(k2_ref[...], q[:, half:], 3, dims=_NT)
    n_top = PEER_TOPK + 1
    v1 = _top_desc(s1, n_top)
    v2 = _top_desc(s2, n_top)
    cands = [v1[a] + v2[b] for a in range(n_top) for b in range(n_top) if (a + 1) * (b + 1) <= n_top]
    pad = 64 - len(cands)
    cand = jnp.concatenate(cands + [jnp.full_like(cands[0], -jnp.inf)] * pad, axis=0)
    top = _top_desc(cand, n_top)
    m = top[0]
    z = top[0] * 0.0
    for j in range(PEER_TOPK):
        z = z + jnp.exp(top[j] - m)
    tau = 0.5 * (top[PEER_TOPK - 1] + top[PEER_TOPK])
    inv_z = 1.0 / z
    e1_ref[0] = jnp.exp(s1 - v1[0])
    e2_ref[0] = jnp.exp(s2 - v2[0]) * inv_z
    th_ref[0] = jnp.exp(tau - m) * inv_z


def _peer_dense_kernel(x_ref, u_ref, vt_ref, e1_ref, e2_ref, th_ref, o_ref, act_ref, a_ref, *, heads, nkeys):
    @pl.when(pl.program_id(1) == 0)
    def _():
        o_ref[...] = jnp.zeros_like(o_ref)

    n_sub = u_ref.shape[0] // nkeys
    n_part = 2
    part = u_ref.shape[0] // n_part
    lanes = 128

    def activations(pt):
        rows = slice(pt * part, (pt + 1) * part)
        act_ref[rows, :] = _dot(u_ref[rows, :], x_ref[...], dims=_NT)

    def gated(i):
        rows = slice(i * nkeys, (i + 1) * nkeys)
        for c in range(x_ref.shape[0] // lanes):
            cols = slice(c * lanes, (c + 1) * lanes)
            gate = None
            for h in range(heads):
                p = e1_ref[h, i:i + 1, cols] * e2_ref[h, :, cols]
                sel = jnp.where(p >= th_ref[h, :, cols], p, 0.0)
                gate = sel if gate is None else gate + sel
            act = act_ref[rows, cols]
            gelu = 0.5 * act * (1.0 + lax.erf(act * (2.0 ** -0.5)))
            a_ref[rows, cols] = (gate * gelu).astype(a_ref.dtype)

    activations(0)
    for pt in range(n_part):
        if pt + 1 < n_part:
            activations(pt + 1)
        for i in range(pt * n_sub // n_part, (pt + 1) * n_sub // n_part):
            gated(i)
        rows = slice(pt * part, (pt + 1) * part)
        o_ref[...] += jnp.dot(vt_ref[:, rows], a_ref[rows, :], preferred_element_type=_F32)


def _peer(x, wq, k1, k2, u_bf, vt_bf):
    t, d = x.shape
    heads, nkeys = PEER_HEADS, PEER_NKEYS
    qdim = wq.shape[1] // heads
    q = _mm(x, wq.astype(_BF16), name="peer_q")
    tt = _tile(t, 512, 128)
    e_shape = jax.ShapeDtypeStruct((heads, nkeys, t), _F32)
    e_blk = pl.BlockSpec((1, nkeys, tt), lambda i, h: (h, 0, i))
    th_blk = pl.BlockSpec((1, 1, tt), lambda i, h: (h, 0, i))
    e1, e2, th = pl.pallas_call(
        _peer_route_kernel,
        grid=(t // tt, heads),
        in_specs=[pl.BlockSpec((tt, qdim), lambda i, h: (i, h)),
                  pl.BlockSpec((nkeys, qdim // 2), lambda i, h: (0, 0)),
                  pl.BlockSpec((nkeys, qdim // 2), lambda i, h: (0, 0))],
        out_specs=[e_blk, e_blk, th_blk],
        out_shape=[e_shape, e_shape, jax.ShapeDtypeStruct((heads, 1, t), _F32)],
        compiler_params=_cp("parallel", "parallel"),
        name="peer_route",
    )(q, k1, k2)
    tb = _tile(t, 512, 128)
    n_sub = 8
    eb = n_sub * nkeys
    xb = x.astype(_BF16)
    return pl.pallas_call(
        functools.partial(_peer_dense_kernel, heads=heads, nkeys=nkeys),
        grid=(t // tb, nkeys // n_sub),
        in_specs=[pl.BlockSpec((tb, d), lambda i, e: (i, 0)),
                  pl.BlockSpec((eb, d), lambda i, e: (e, 0)),
                  pl.BlockSpec((d, eb), lambda i, e: (0, e)),
                  pl.BlockSpec((heads, n_sub, tb), lambda i, e: (0, e, i)),
                  pl.BlockSpec((heads, nkeys, tb), lambda i, e: (0, 0, i)),
                  pl.BlockSpec((heads, 1, tb), lambda i, e: (0, 0, i))],
        out_specs=pl.BlockSpec((d, tb), lambda i, e: (0, i)),
        out_shape=jax.ShapeDtypeStruct((d, t), _F32),
        scratch_shapes=[pltpu.VMEM((eb, tb), _F32), pltpu.VMEM((eb, tb), _BF16)],
        compiler_params=_cp("parallel", "arbitrary"),
        name="peer_dense",
    )(xb, u_bf, vt_bf, e1, e2, th)


def _hyena_na_mixer(h_lat, h_ctx, mod_l, mod_c, w_in, conv_w, conv_b, fparams, skip, rpb, w_out):
    split = conv_w.shape[1]
    wb = w_in.astype(_BF16)
    w_hy, w_na = wb[:, :split], wb[:, split:]
    outs = []
    qkvs = []
    for hh, (sc, sh) in ((h_lat, mod_l), (h_ctx, mod_c)):
        p_hy = _mm(hh, w_hy, scale=sc, shift=sh, name="hy_proj")
        qkvs.append(_mm(hh, w_na, scale=sc, shift=sh, out_dtype=_BF16, name="na_proj"))
        outs.append(_hyena(p_hy, conv_w, conv_b, fparams, skip))
    na_lat = _natten(qkvs[0], qkvs[1], rpb)
    na_ctx = _ctx_attn(qkvs[1])
    wo = w_out.astype(_BF16)
    y_lat = _mm(jnp.concatenate([outs[0], na_lat], axis=-1), wo, name="ab_out")
    y_ctx = _mm(jnp.concatenate([outs[1], na_ctx], axis=-1), wo, name="ab_out")
    return y_lat, y_ctx


def kernel(x, c, ctx, c_ctx, ada_w, ada_b, ln_w, ln_b, ab_w_in, hy_conv_w, hy_conv_b, hy_fw1, hy_fb1, hy_fw2, hy_fb2, hy_fw3, hy_fb3, hy_fw4, hy_skip, na_rpb, ab_w_out, ml_w_in, ml_b_gates, ml_conv_w, ml_conv_b, ml_norm_w, ml_w_out, peer_wq, peer_k1, peer_k2, peer_u, peer_v):
    bsz, seq_len, d = x.shape
    depth = ada_w.shape[0]
    assert bsz == 1 and depth == 2, "one even (Hyena/attention) layer followed by one last odd (mLSTM) layer"
    alpha = (2 * depth) ** 0.25
    h_lat, h_ctx = x[0], ctx[0]
    cvec = jnp.zeros((8, d), _F32).at[0].set(c[0]).at[1].set(c_ctx)
    row = lambda a: a.reshape(1, d)
    for layer in range(depth):
        last = layer == depth - 1
        mod = _mm(cvec, ada_w[layer], pre="silu", bias=ada_b[layer].reshape(1, 6 * d), passes=3, name="ada_mod")
        sh_l, sc_l, g_l, shf_l, scf_l, gf_l = [mod[0:1, i * d:(i + 1) * d] for i in range(6)]
        sh_c, sc_c, g_c, shf_c, scf_c, gf_c = [mod[1:2, i * d:(i + 1) * d] for i in range(6)]
        if layer % 2 == 0:
            fparams = (hy_fw1[0], hy_fb1[0], hy_fw2[0], hy_fb2[0], hy_fw3[0], hy_fb3[0], hy_fw4[0])
            y_lat, y_ctx = _hyena_na_mixer(h_lat, h_ctx, (sc_l, sh_l), (sc_c, sh_c), ab_w_in[0], hy_conv_w[0],
                                           hy_conv_b[0], fparams, hy_skip[0], na_rpb[0], ab_w_out[0])
        else:
            y_lat = _mlstm_mixer(h_lat, h_ctx, (sc_l, sh_l), (sc_c, sh_c), ml_w_in[0], ml_b_gates[0], ml_conv_w[0],
                                 ml_conv_b[0], ml_norm_w[0], ml_w_out[0])
            y_ctx = None
        lw0, lb0, lw1, lb1 = row(ln_w[layer, 0]), row(ln_b[layer, 0]), row(ln_w[layer, 1]), row(ln_b[layer, 1])
        u_bf = peer_u[layer].astype(_BF16)
        vt_bf = peer_v[layer].astype(_BF16).T
        peer = lambda v: _peer(v, peer_wq[layer], peer_k1[layer], peer_k2[layer], u_bf, vt_bf)
        h_lat, v_lat = _ln_res(h_lat, y_lat, g_l, lw0, lb0, scf_l, shf_l, alpha=alpha, with_mod=True)
        h_lat, _ = _ln_res(h_lat, peer(v_lat), gf_l, lw1, lb1, scf_l, shf_l, alpha=alpha, with_mod=False,
                           y_transposed=True)
        if not last:
            h_ctx, v_ctx = _ln_res(h_ctx, y_ctx, g_c, lw0, lb0, scf_c, shf_c, alpha=alpha, with_mod=True)
            h_ctx, _ = _ln_res(h_ctx, peer(v_ctx), gf_c, lw1, lb1, scf_c, shf_c, alpha=alpha, with_mod=False,
                               y_transposed=True)
    return h_lat[None]
```

```python
import functools
import math

import numpy as np
import jax
import jax.numpy as jnp
from jax import lax
from jax.experimental import pallas as pl
from jax.experimental.pallas import tpu as pltpu

_F32 = jnp.float32
_BF16 = jnp.bfloat16
_FP8 = jnp.float8_e4m3fn
_FP8_MAX = 448.0

GRID_W = 64
HY_ORDER = 2
HY_EMB = 33
HY_DECAY_TARGET = 1e-2
HY_FAST_PCT = 0.3
HY_SLOW_PCT = 1.5
NA_HEADS = 8
NA_WIN_ROWS = 8
NA_WIN_COLS = 16
ML_HEADS = 8
ML_TILE = 256
ML_M0 = -1e30
ROPE_BASE = 10000.0
PEER_HEADS = 8
PEER_NKEYS = 128
PEER_TOPK = 16
LN_EPS = 1e-6

_VMEM_LIMIT = 52 * 1024 * 1024
_NEG = -1e30
_FFT_DT = jnp.bfloat16

_NT = (((1,), (1,)), ((), ()))
_TN = (((0,), (0,)), ((), ()))


def _cp(*sem):
    return pltpu.CompilerParams(dimension_semantics=sem, vmem_limit_bytes=_VMEM_LIMIT)


def _tile(n, cap, align):
    t = (min(cap, n) // align) * align
    while t >= align:
        if n % t == 0:
            return t
        t -= align
    return n


def _split(x):
    hi = x.astype(_BF16)
    lo = (x.astype(_F32) - hi.astype(_F32)).astype(_BF16)
    return hi, lo


def _dot(a, b, passes=1, dims=None):
    dn = dims or (((a.ndim - 1,), (0,)), ((), ()))
    f = lambda x, y: lax.dot_general(x, y, dn, preferred_element_type=_F32)
    if passes == 1:
        return f(a.astype(_BF16), b.astype(_BF16))
    ah, al = _split(a)
    bh, bl = _split(b)
    return f(ah, bh) + (f(ah, bl) + f(al, bh))


def _sigmoid(x):
    return 1.0 / (1.0 + jnp.exp(-x))


def _mm_kernel(*refs, pre, has_bias, passes):
    it = iter(refs)
    a_ref = next(it)
    if pre == "mod":
        sc_ref, sh_ref = next(it), next(it)
    b_ref = next(it)
    bias_ref = next(it) if has_bias else None
    o_ref = next(it)
    a = a_ref[...]
    if pre == "mod":
        a = a.astype(_F32) * (1.0 + sc_ref[...]) + sh_ref[...]
    elif pre == "silu":
        a = a * _sigmoid(a)
    acc = _dot(a, b_ref[...], passes)
    if has_bias:
        acc = acc + bias_ref[...]
    o_ref[...] = acc.astype(o_ref.dtype)


def _mm(a, b, *, scale=None, shift=None, pre=None, bias=None, out_dtype=_F32, passes=1, name="mm"):
    m, k = a.shape
    n = b.shape[1]
    if scale is not None:
        pre = "mod"
    tile_bytes = 4 * 1024 * 1024
    tm = _tile(m, max(8, min(1024, 2 * tile_bytes // (k * a.dtype.itemsize))), 8)
    tn = _tile(n, max(128, min(8192, tile_bytes // (k * b.dtype.itemsize), tile_bytes // (tm * 4))), 128)
    in_specs = [pl.BlockSpec((tm, k), lambda i, j: (i, 0))]
    args = [a]
    if pre == "mod":
        in_specs += [pl.BlockSpec((1, k), lambda i, j: (0, 0))] * 2
        args += [scale, shift]
    in_specs.append(pl.BlockSpec((k, tn), lambda i, j: (0, j)))
    args.append(b)
    if bias is not None:
        in_specs.append(pl.BlockSpec((1, tn), lambda i, j: (0, j)))
        args.append(bias)
    return pl.pallas_call(
        functools.partial(_mm_kernel, pre=pre, has_bias=bias is not None, passes=passes),
        grid=(m // tm, n // tn),
        in_specs=in_specs,
        out_specs=pl.BlockSpec((tm, tn), lambda i, j: (i, j)),
        out_shape=jax.ShapeDtypeStruct((m, n), out_dtype),
        compiler_params=_cp("parallel", "parallel"),
        name=name,
    )(*args)


def _ln_res_kernel(h_ref, y_ref, g_ref, w_ref, b_ref, sc_ref, sh_ref, *o_refs, alpha, with_mod, y_transposed):
    y = y_ref[...].T if y_transposed else y_ref[...]
    z = alpha * h_ref[...] + g_ref[...] * y
    mu = jnp.mean(z, axis=-1, keepdims=True)
    zc = z - mu
    var = jnp.mean(zc * zc, axis=-1, keepdims=True)
    hn = zc * lax.rsqrt(var + LN_EPS) * w_ref[...] + b_ref[...]
    o_refs[0][...] = hn
    if with_mod:
        o_refs[1][...] = hn * (1.0 + sc_ref[...]) + sh_ref[...]


def _ln_res(h, y, g, w, b, sc, sh, *, alpha, with_mod, y_transposed=False):
    m, d = h.shape
    tr = _tile(m, 256, 128 if y_transposed else 8)
    row = pl.BlockSpec((tr, d), lambda i: (i, 0))
    y_spec = pl.BlockSpec((d, tr), lambda i: (0, i)) if y_transposed else row
    vec = pl.BlockSpec((1, d), lambda i: (0, 0))
    n_out = 2 if with_mod else 1
    out = pl.pallas_call(
        functools.partial(_ln_res_kernel, alpha=alpha, with_mod=with_mod, y_transposed=y_transposed),
        grid=(m // tr,),
        in_specs=[row, y_spec, vec, vec, vec, vec, vec],
        out_specs=[row] * n_out,
        out_shape=[jax.ShapeDtypeStruct((m, d), _F32)] * n_out,
        compiler_params=_cp("parallel"),
        name="ln_res",
    )(h, y, g, w, b, sc, sh)
    return out if with_mod else (out[0], None)


def _conv3_kernel(*refs, n_out, act, rope, out_scales):
    it = iter(refs)
    xp_ref, x_ref, xn_ref, w_ref, b_ref = (next(it) for _ in range(5))
    if rope:
        cos_ref, sin_ref = next(it), next(it)
    o_refs = [next(it) for _ in range(n_out)]
    i = pl.program_id(0)
    x = x_ref[...]
    tr = x.shape[0]
    rows = lax.broadcasted_iota(jnp.int32, x.shape, 0)
    prev_row = jnp.where(i > 0, xp_ref[7:8, :], 0.0)
    next_row = jnp.where(i < pl.num_programs(0) - 1, xn_ref[0:1, :], 0.0)
    x_m1 = jnp.where(rows == 0, prev_row, pltpu.roll(x, 1, axis=0))
    x_p1 = jnp.where(rows == tr - 1, next_row, pltpu.roll(x, tr - 1, axis=0))
    y = b_ref[...] + x_m1 * w_ref[0:1, :] + x * w_ref[1:2, :] + x_p1 * w_ref[2:3, :]
    if act:
        y = y * _sigmoid(y)
    wo = y.shape[1] // n_out
    for o in range(n_out):
        yo = y[:, o * wo:(o + 1) * wo]
        if rope:
            cos, sin = cos_ref[...], sin_ref[...]
            lane = lax.broadcasted_iota(jnp.int32, cos.shape, 1)
            first = (lane % 64) < 32
            parts = []
            for hh in range(wo // 128):
                yh = yo[:, hh * 128:(hh + 1) * 128]
                sw = jnp.where(first, pltpu.roll(yh, 96, axis=1), pltpu.roll(yh, 32, axis=1))
                parts.append(yh * cos + sw * sin)
            yo = jnp.concatenate(parts, axis=1)
        if out_scales[o] != 1.0:
            yo = yo * out_scales[o]
        o_refs[o][...] = yo.astype(o_refs[o].dtype)


def _conv3(x, w, b, *, n_out, out_dtype, act=False, rope=None, out_scales=None):
    l, cx = x.shape
    tr = _tile(l, 256, 8)
    nb8 = l // 8
    out_scales = out_scales or (1.0,) * n_out
    in_specs = [
        pl.BlockSpec((8, cx), lambda i: (jnp.maximum(i * (tr // 8) - 1, 0), 0)),
        pl.BlockSpec((tr, cx), lambda i: (i, 0)),
        pl.BlockSpec((8, cx), lambda i: (jnp.minimum((i + 1) * (tr // 8), nb8 - 1), 0)),
        pl.BlockSpec((3, cx), lambda i: (0, 0)),
        pl.BlockSpec((1, cx), lambda i: (0, 0)),
    ]
    args = [x, x, x, w, b.reshape(1, cx)]
    if rope is not None:
        in_specs += [pl.BlockSpec((tr, 128), lambda i: (i, 0))] * 2
        args += list(rope)
    wo = cx // n_out
    return pl.pallas_call(
        functools.partial(_conv3_kernel, n_out=n_out, act=act, rope=rope is not None, out_scales=out_scales),
        grid=(l // tr,),
        in_specs=in_specs,
        out_specs=[pl.BlockSpec((tr, wo), lambda i: (i, 0))] * n_out,
        out_shape=[jax.ShapeDtypeStruct((l, wo), out_dtype)] * n_out,
        compiler_params=_cp("parallel"),
        name="conv3",
    )(*args)


def _hy_filter_kernel(f_ref, t_ref, w1_ref, b1_ref, w2_ref, b2_ref, w3_ref, b3_ref, w4_ref, d_ref, o_ref):
    h = jnp.sin(_dot(f_ref[...], w1_ref[...], 3) + b1_ref[...])
    h = jnp.sin(_dot(h, w2_ref[...], 3) + b2_ref[...])
    h = jnp.sin(_dot(h, w3_ref[...], 3) + b3_ref[...])
    taps = _dot(h, w4_ref[...], 3)
    tv = t_ref[...]
    window = jnp.exp(-tv[:, 0:1] * d_ref[...]) * tv[:, 1:2]
    c = window.shape[1]
    for g in range(taps.shape[1] // c):
        o_ref[:, g * c:(g + 1) * c] = (taps[:, g * c:(g + 1) * c] * window).astype(o_ref.dtype)


def _hy_taps(seq_len, l_pad, fw1, fb1, fw2, fb2, fw3, fb3, fw4, width):
    hid = fw1.shape[1]
    n = np.arange(2 * l_pad)
    off = np.where(n < l_pad, n, 2 * l_pad - n)
    exists = np.where(n < l_pad, off < seq_len, (off >= 1) & (off < seq_len))
    pos = jnp.asarray(np.where(exists, off, 0), _F32)
    t = pos / (seq_len - 1)
    omega = 2.0 * math.pi * pos / seq_len
    t = jnp.stack([t, jnp.asarray(exists, _F32)], axis=-1)
    bands = (HY_EMB - 1) // 2
    freqs = jnp.linspace(1e-4, bands - 1, bands, dtype=_F32)
    ang = omega[:, None] * freqs[None, :]
    kpad = 64
    feats = jnp.concatenate([t[:, 0:1], jnp.cos(ang), -jnp.sin(ang), jnp.zeros((2 * l_pad, kpad - HY_EMB), _F32)],
                            axis=-1)
    w1 = jnp.concatenate([fw1, jnp.zeros((kpad - HY_EMB, hid), _F32)], axis=0)
    max_decay = math.log(HY_DECAY_TARGET) / HY_FAST_PCT
    min_decay = math.log(HY_DECAY_TARGET) / HY_SLOW_PCT
    deltas = jnp.abs(jnp.linspace(min_decay, max_decay, width, dtype=_F32))[None, :]
    n_cols = fw4.shape[1] // 2
    tl = _tile(l_pad, 256, 8)
    per_dir = l_pad // tl
    full = lambda shape: pl.BlockSpec(shape, lambda i: (0,) * len(shape))
    return pl.pallas_call(
        _hy_filter_kernel,
        grid=(2 * per_dir,),
        in_specs=[pl.BlockSpec((tl, kpad), lambda i: (i, 0)), pl.BlockSpec((tl, 2), lambda i: (i, 0)),
                  full((kpad, hid)), full((1, hid)), full((hid, hid)), full((1, hid)), full((hid, hid)), full((1, hid)),
                  pl.BlockSpec((hid, n_cols), lambda i: (0, i // per_dir)), full((1, width))],
        out_specs=pl.BlockSpec((tl, n_cols), lambda i: (i, 0)),
        out_shape=jax.ShapeDtypeStruct((2 * l_pad, n_cols), _BF16),
        compiler_params=_cp("parallel"),
        name="hy_filter",
    )(feats, t, w1, fb1.reshape(1, hid), fw2, fb2.reshape(1, hid), fw3, fb3.reshape(1, hid), fw4, deltas)


def _fft_consts(n1, n2):
    n = n1 * n2
    k1 = np.arange(n1)[:, None]
    j1 = np.arange(n1 // 2)[None, :]
    ang1 = 2.0 * np.pi * k1 * j1 / n1
    f1_fwd = np.concatenate([np.cos(ang1), -np.sin(ang1)], axis=0)
    f1_inv = np.concatenate([np.cos(ang1).T, -np.sin(ang1).T], axis=1) / n
    k2 = np.arange(n2)[:, None]
    j2 = np.arange(n2)[None, :]
    ang2 = 2.0 * np.pi * k2 * j2 / n2
    f2 = np.stack([np.cos(ang2), -np.sin(ang2)])
    angt = 2.0 * np.pi * np.arange(n1)[:, None] * np.arange(n2)[None, :] / n
    tw = np.stack([np.cos(angt), -np.sin(angt)], axis=1)
    angf = 2.0 * np.pi * k1 * np.arange(n1)[None, :] / n1
    f1_full = np.concatenate([np.cos(angf), -np.sin(angf)], axis=0)
    as32 = lambda a: jnp.asarray(a, dtype=_F32)
    return as32(f1_fwd), as32(f1_inv), as32(f2), as32(tw), as32(np.transpose(tw, (0, 2, 1))), as32(f1_full)


def _cmm(f_stack, xr, xi):
    n = f_stack.shape[0] // 2
    p1 = jnp.dot(f_stack, xr.astype(_BF16), preferred_element_type=_F32)
    p2 = jnp.dot(f_stack, xi.astype(_BF16), preferred_element_type=_F32)
    return p1[:n] - p2[n:], p2[:n] + p1[n:]


def _twiddled_dft(f2_ref, tw_ref):
    f2r, f2i = f2_ref[0], f2_ref[1]
    tr, ti = tw_ref[0, 0:1, :], tw_ref[0, 1:2, :]
    return jnp.concatenate([f2r * tr - f2i * ti, f2r * ti + f2i * tr], axis=0).astype(_BF16)


def _fft_spec_kernel(a_ref, f2_ref, tw_ref, o_ref):
    fw = _twiddled_dft(f2_ref, tw_ref)
    hr, hi = _cmm(fw, a_ref[0, 0], a_ref[1, 0])
    o_ref[0, 0, 0] = hr.astype(o_ref.dtype)
    o_ref[0, 1, 0] = hi.astype(o_ref.dtype)


def _fft_mid_kernel(a_ref, h_ref, f2_ref, tw_ref, twc_ref, o_ref):
    fw = _twiddled_dft(f2_ref, tw_ref)
    br, bi = _cmm(fw, a_ref[0, 0], a_ref[1, 0])
    hr, hi = h_ref[0, 0, 0].astype(_F32), h_ref[0, 1, 0].astype(_F32)
    yr = (br * hr - bi * hi).astype(_BF16)
    yi = (br * hi + bi * hr).astype(_BF16)
    f2r, f2i = f2_ref[0], f2_ref[1]
    tcr, tci = twc_ref[0, :, 0:1], twc_ref[0, :, 1:2]
    g = jnp.concatenate([tcr * f2r - tci * f2i, -(tcr * f2i + tci * f2r)], axis=0).astype(_BF16)
    zr, zi = _cmm(g, yr, yi)
    o_ref[0, 0] = zr.astype(o_ref.dtype)
    o_ref[1, 0] = zi.astype(o_ref.dtype)


def _fft_shape(l_pad):
    n2 = 256 if l_pad >= 8192 else 64
    n1 = 2 * l_pad // n2
    return n1, n2


def _hy_spectra(taps, width, consts, n1, n2):
    f1_fwd, _, f2, tw, _, f1_full = consts
    half = taps.shape[1]
    a = _mm(f1_full, taps.reshape(n1, n2 * half), out_dtype=_FFT_DT, name="fft_a_filt")
    a = a.reshape(2, n1, n2, half)
    return pl.pallas_call(
        _fft_spec_kernel,
        grid=(n1, HY_ORDER),
        in_specs=[pl.BlockSpec((2, 1, n2, width), lambda k, o: (0, k, 0, o)),
                  pl.BlockSpec((2, n2, n2), lambda k, o: (0, 0, 0)),
                  pl.BlockSpec((1, 2, n2), lambda k, o: (k, 0, 0))],
        out_specs=pl.BlockSpec((1, 2, 1, n2, width), lambda k, o: (o, 0, k, 0, 0)),
        out_shape=jax.ShapeDtypeStruct((HY_ORDER, 2, n1, n2, width), _FFT_DT),
        compiler_params=_cp("parallel", "parallel"),
        name="fft_spec",
    )(a, f2, tw)


def _long_conv(z, spec, order, consts, n1, n2):
    f1_fwd, f1_inv, f2, tw, twc, _ = consts
    l_pad, c = z.shape
    a = _mm(f1_fwd, z.reshape(n1 // 2, n2 * c), out_dtype=_FFT_DT, name="fft_a")
    a = a.reshape(2, n1, n2, c)
    blk = pl.BlockSpec((2, 1, n2, c), lambda k: (0, k, 0, 0))
    mid = pl.pallas_call(
        _fft_mid_kernel,
        grid=(n1,),
        in_specs=[blk, pl.BlockSpec((1, 2, 1, n2, c), lambda k: (order, 0, k, 0, 0)),
                  pl.BlockSpec((2, n2, n2), lambda k: (0, 0, 0)),
                  pl.BlockSpec((1, 2, n2), lambda k: (k, 0, 0)), pl.BlockSpec((1, n2, 2), lambda k: (k, 0, 0))],
        out_specs=blk,
        out_shape=jax.ShapeDtypeStruct((2, n1, n2, c), _FFT_DT),
        compiler_params=_cp("parallel"),
        name="fft_mid",
    )(a, spec, f2, tw, twc)
    y = _mm(f1_inv, mid.reshape(2 * n1, n2 * c), name="fft_c")
    return y.reshape(l_pad, c)


def _hy_gate_kernel(y_ref, z_ref, s_ref, g_ref, o_ref):
    o_ref[...] = (g_ref[...] * (y_ref[...] + z_ref[...] * s_ref[...])).astype(o_ref.dtype)


def _hy_gate(y, z, skip, gate, out_dtype):
    l, c = y.shape
    tr = _tile(l, 512, 8)
    row = pl.BlockSpec((tr, c), lambda i: (i, 0))
    return pl.pallas_call(
        _hy_gate_kernel,
        grid=(l // tr,),
        in_specs=[row, row, pl.BlockSpec((1, c), lambda i: (0, 0)), row],
        out_specs=row,
        out_shape=jax.ShapeDtypeStruct((l, c), out_dtype),
        compiler_params=_cp("parallel"),
        name="hy_gate",
    )(y, z, skip.reshape(1, c), gate)


def _hyena(p, conv_w, conv_b, fparams, skip):
    l = p.shape[0]
    width = p.shape[1] // (HY_ORDER + 1)
    l_pad = max(l, 1024)
    n1, n2 = _fft_shape(l_pad)
    consts = _fft_consts(n1, n2)
    taps = _hy_taps(l, l_pad, *fparams, width)
    parts = _conv3(p, conv_w, conv_b, n_out=HY_ORDER + 1, out_dtype=_F32)
    if l_pad != l:
        parts = [jnp.pad(a, ((0, l_pad - l), (0, 0))) for a in parts]
    spec = _hy_spectra(taps, width, consts, n1, n2)
    z = parts[0]
    for o in range(HY_ORDER):
        y = _long_conv(z, spec, o, consts, n1, n2)
        z = _hy_gate(y, z, skip[o], parts[o + 1], _F32 if o + 1 < HY_ORDER else _BF16)
    return z[:l]


def _na_bias_table(rpb, w):
    kr, kc = NA_WIN_ROWS, NA_WIN_COLS
    cols = np.arange(w)
    start = np.clip(cols - kc // 2, 0, w - kc)
    kcol = np.arange(w)[None, :]
    in_win = (kcol >= start[:, None]) & (kcol < start[:, None] + kc)
    rp = jnp.pad(rpb, ((0, 0), (0, 0), (w - 1, w)))
    by_q = jnp.stack([rp[:, :, kc - 1 + w - 1 - q:kc - 1 + 2 * w - 1 - q] for q in range(w)], axis=2)
    by_q = jnp.where(in_win[None, None], by_q, _NEG)
    t = jnp.stack([jnp.concatenate([by_q[:, i - d + kr - 1] for i in range(kr)], axis=-1) for d in range(kr)], axis=1)
    return t


def _natten_kernel(q_ref, k_ref, v_ref, kc_ref, vc_ref, bias_ref, o_ref, *, rows, w, scale):
    qb = pl.program_id(1)
    kr = NA_WIN_ROWS
    kc, vc = kc_ref[...], vc_ref[...]
    for a in range(kr):
        r = qb * kr + a
        r0 = jnp.clip(r - kr // 2, 0, rows - kr)
        d = r - r0
        start = pl.multiple_of(r0 * w, w)
        kwin = k_ref[pl.ds(start, kr * w), :]
        vwin = v_ref[pl.ds(start, kr * w), :]
        qa = q_ref[a * w:(a + 1) * w, :]
        s_lat = _dot(qa, kwin, dims=_NT) * scale + bias_ref[0, d]
        s_ctx = _dot(qa, kc, dims=_NT) * scale
        m = jnp.maximum(jnp.max(s_lat, axis=-1, keepdims=True), jnp.max(s_ctx, axis=-1, keepdims=True))
        p_lat = jnp.exp(s_lat - m)
        p_ctx = jnp.exp(s_ctx - m)
        den = jnp.sum(p_lat, axis=-1, keepdims=True) + jnp.sum(p_ctx, axis=-1, keepdims=True)
        o = _dot(p_lat, vwin) + _dot(p_ctx, vc)
        o_ref[a * w:(a + 1) * w, :] = (o / den).astype(o_ref.dtype)


def _natten(qkv, kv_ctx, rpb):
    l = qkv.shape[0]
    h = NA_HEADS
    hd = qkv.shape[1] // (3 * h)
    w = GRID_W
    rows = l // w
    kr = NA_WIN_ROWS
    assert rows % kr == 0 and rows >= kr and hd % 128 == 0
    lc = kv_ctx.shape[0]
    bias = _na_bias_table(rpb, w)
    tq = kr * w
    return pl.pallas_call(
        functools.partial(_natten_kernel, rows=rows, w=w, scale=hd ** -0.5),
        grid=(h, rows // kr),
        in_specs=[pl.BlockSpec((tq, hd), lambda hh, qb: (qb, hh)),
                  pl.BlockSpec((l, hd), lambda hh, qb: (0, h + hh)),
                  pl.BlockSpec((l, hd), lambda hh, qb: (0, 2 * h + hh)),
                  pl.BlockSpec((lc, hd), lambda hh, qb: (0, h + hh)),
                  pl.BlockSpec((lc, hd), lambda hh, qb: (0, 2 * h + hh)),
                  pl.BlockSpec((1, kr, w, kr * w), lambda hh, qb: (hh, 0, 0, 0))],
        out_specs=pl.BlockSpec((tq, hd), lambda hh, qb: (qb, hh)),
        out_shape=jax.ShapeDtypeStruct((l, h * hd), _BF16),
        compiler_params=_cp("parallel", "parallel"),
        name="natten",
    )(qkv, qkv, qkv, kv_ctx, kv_ctx, bias)


def _ctx_attn_kernel(q_ref, k_ref, v_ref, o_ref, *, scale):
    s = _dot(q_ref[...], k_ref[...], dims=_NT) * scale
    p = jnp.exp(s - jnp.max(s, axis=-1, keepdims=True))
    den = jnp.sum(p, axis=-1, keepdims=True)
    o_ref[...] = (_dot(p, v_ref[...]) / den).astype(o_ref.dtype)


def _ctx_attn(qkv):
    lc = qkv.shape[0]
    h = NA_HEADS
    hd = qkv.shape[1] // (3 * h)
    blk = lambda off: pl.BlockSpec((lc, hd), lambda hh: (0, off + hh))
    return pl.pallas_call(
        functools.partial(_ctx_attn_kernel, scale=hd ** -0.5),
        grid=(h,),
        in_specs=[blk(0), blk(h), blk(2 * h)],
        out_specs=blk(0),
        out_shape=jax.ShapeDtypeStruct((lc, h * hd), _BF16),
        compiler_params=_cp("parallel"),
        name="ctx_attn",
    )(qkv, qkv, qkv)


def _ml_gates_kernel(g_ref, b_ref, o_ref, *, heads):
    g = g_ref[...] + b_ref[...]
    logsig = jnp.minimum(g, 0.0) - jnp.log(1.0 + jnp.exp(-jnp.abs(g)))
    lane = lax.broadcasted_iota(jnp.int32, g.shape, 1)
    is_forget = (lane // heads) % 2 == 1
    o_ref[...] = jnp.where(is_forget, logsig, g)


def _ml_gates(g_raw, b_gates):
    l, n = g_raw.shape
    tr = _tile(l, 1024, 8)
    bias = jnp.pad(b_gates, (0, n - b_gates.shape[0])).reshape(1, n)
    return pl.pallas_call(
        functools.partial(_ml_gates_kernel, heads=ML_HEADS),
        grid=(l // tr,),
        in_specs=[pl.BlockSpec((tr, n), lambda i: (i, 0)), pl.BlockSpec((1, n), lambda i: (0, 0))],
        out_specs=pl.BlockSpec((tr, n), lambda i: (i, 0)),
        out_shape=jax.ShapeDtypeStruct((l, n), _F32),
        compiler_params=_cp("parallel"),
        name="ml_gates",
    )(g_raw, bias)


def _ml_scan_kernel(*refs, heads, dk, dv, reverse, with_h):
    it = iter(refs)
    if with_h:
        q_ref = next(it)
    k_ref, v_ref, g_ref, gt_ref, c0_ref, n0_ref, m0_ref = (next(it) for _ in range(7))
    if with_h:
        h_ref = next(it)
    c_ref, n_ref, m_ref = next(it), next(it), next(it)

    @pl.when(pl.program_id(0) == 0)
    def _():
        c_ref[...] = c0_ref[...]
        n_ref[...] = n0_ref[...]
        m_ref[...] = m0_ref[...]

    t = k_ref.shape[0]
    ti = lax.broadcasted_iota(jnp.int32, (t, t), 0)
    si = lax.broadcasted_iota(jnp.int32, (t, t), 1)
    seen = (si >= ti) if reverse else (si <= ti)
    seen_t = (ti >= si) if reverse else (ti <= si)
    g = g_ref[...]
    gt = gt_ref[0]
    c0 = 2 * heads if reverse else 0
    for h in range(heads):
        i_col = g[:, c0 + h:c0 + h + 1]
        f_col = g[:, c0 + heads + h:c0 + heads + h + 1]
        i_row = gt[c0 + h:c0 + h + 1, :]
        f_row = gt[c0 + heads + h:c0 + heads + h + 1, :]
        b_col = jnp.sum(jnp.where(seen, f_row, 0.0), axis=1, keepdims=True)
        b_row = jnp.sum(jnp.where(seen_t, f_col, 0.0), axis=0, keepdims=True)
        b_end = jnp.sum(f_row, axis=1, keepdims=True)
        m_prev = m_ref[h][:, 0:1]
        w_log_row = b_end - b_row + i_row
        w_log_col = b_end - b_col + i_col
        m_new = jnp.maximum(b_end + m_prev, jnp.max(w_log_row, axis=1, keepdims=True))
        decay = jnp.exp(b_end + m_prev - m_new)
        w_col = jnp.exp(w_log_col - m_new)
        kh = k_ref[:, h * dk:(h + 1) * dk]
        vh = v_ref[:, h * dv:(h + 1) * dv]
        ct = c_ref[h]
        nv = n_ref[h]
        if with_h:
            qh = q_ref[:, h * dk:(h + 1) * dk]
            log_d = jnp.where(seen, b_col - b_row + i_row, -jnp.inf)
            m_in = b_col + m_prev
            m_t = jnp.maximum(m_in, jnp.max(log_d, axis=1, keepdims=True))
            sc = _dot(qh, kh, dims=_NT) * jnp.exp(log_d - m_t)
            g_in = jnp.exp(m_in - m_t)
            num = _dot(sc, vh) + _dot(qh, ct) * g_in
            den = jnp.sum(sc, axis=1, keepdims=True) + jnp.sum(qh.astype(_F32) * nv, axis=1, keepdims=True) * g_in
            h_ref[:, h * dv:(h + 1) * dv] = num / jnp.maximum(jnp.abs(den), jnp.exp(-m_t))
        vw = vh.astype(_F32) * w_col
        c_ref[h] = decay * ct + _dot(kh, vw, dims=_TN)
        n_ref[h] = decay * nv + jnp.sum(kh.astype(_F32) * w_col, axis=0, keepdims=True)
        m_ref[h] = jnp.broadcast_to(m_new, m_ref.shape[1:])


def _ml_scan(q, k, v, gates, state, *, reverse, dk, dv):
    l = k.shape[0]
    heads = ML_HEADS
    t = min(ML_TILE, l)
    assert l % t == 0
    nc = l // t
    with_h = q is not None
    gt = jnp.transpose(gates.reshape(nc, t, gates.shape[1]), (0, 2, 1))
    cidx = (lambda j: nc - 1 - j) if reverse else (lambda j: j)
    row = lambda width: pl.BlockSpec((t, width), lambda j: (cidx(j), 0))
    st_specs = [pl.BlockSpec((heads, dk, dv), lambda j: (0, 0, 0)), pl.BlockSpec((heads, 1, dk), lambda j: (0, 0, 0)),
                pl.BlockSpec((heads, 1, 128), lambda j: (0, 0, 0))]
    st_shapes = [jax.ShapeDtypeStruct((heads, dk, dv), _F32), jax.ShapeDtypeStruct((heads, 1, dk), _F32),
                 jax.ShapeDtypeStruct((heads, 1, 128), _F32)]
    in_specs = ([row(heads * dk)] if with_h else []) + [
        row(heads * dk), row(heads * dv), row(gates.shape[1]),
        pl.BlockSpec((1, gates.shape[1], t), lambda j: (cidx(j), 0, 0))] + st_specs
    args = ([q] if with_h else []) + [k, v, gates, gt, *state]
    out_specs = ([row(heads * dv)] if with_h else []) + st_specs
    out_shape = ([jax.ShapeDtypeStruct((l, heads * dv), _F32)] if with_h else []) + st_shapes
    out = pl.pallas_call(
        functools.partial(_ml_scan_kernel, heads=heads, dk=dk, dv=dv, reverse=reverse, with_h=with_h),
        grid=(nc,),
        in_specs=in_specs,
        out_specs=out_specs,
        out_shape=out_shape,
        compiler_params=_cp("arbitrary"),
        name="ml_scan",
    )(*args)
    return (out[0], tuple(out[1:])) if with_h else (None, tuple(out))


def _ml_merge_kernel(hf_ref, hb_ref, o_ref, w_ref, out_ref, *, heads, dv):
    for h in range(heads):
        sl = slice(h * dv, (h + 1) * dv)
        x = hf_ref[:, sl] + hb_ref[:, sl]
        mu = jnp.mean(x, axis=-1, keepdims=True)
        xc = x - mu
        var = jnp.mean(xc * xc, axis=-1, keepdims=True)
        hn = xc * lax.rsqrt(var + LN_EPS) * w_ref[:, sl]
        out_ref[:, sl] = (hn * _sigmoid(o_ref[:, sl])).astype(out_ref.dtype)


def _ml_merge(hf, hb, o, norm_w, dv):
    l, n = hf.shape
    tr = _tile(l, 256, 8)
    row = pl.BlockSpec((tr, n), lambda i: (i, 0))
    return pl.pallas_call(
        functools.partial(_ml_merge_kernel, heads=ML_HEADS, dv=dv),
        grid=(l // tr,),
        in_specs=[row, row, row, pl.BlockSpec((1, n), lambda i: (0, 0))],
        out_specs=row,
        out_shape=jax.ShapeDtypeStruct((l, n), _BF16),
        compiler_params=_cp("parallel"),
        name="ml_merge",
    )(hf, hb, o, norm_w.reshape(1, n))


def _rope_tables(l):
    quarter = 32
    pos = jnp.arange(l)
    inv = ROPE_BASE ** (-jnp.arange(quarter, dtype=_F32) / quarter)
    ang_r = (pos // GRID_W).astype(_F32)[:, None] * inv[None]
    ang_c = (pos % GRID_W).astype(_F32)[:, None] * inv[None]
    cos = jnp.concatenate([jnp.cos(ang_r)] * 2 + [jnp.cos(ang_c)] * 2, axis=-1)
    sin = jnp.concatenate([-jnp.sin(ang_r), jnp.sin(ang_r), -jnp.sin(ang_c), jnp.sin(ang_c)], axis=-1)
    return cos, sin


def _mlstm_mixer(h_lat, h_ctx, mod_l, mod_c, w_in, b_gates, conv_w, conv_b, norm_w, w_out):
    heads = ML_HEADS
    v_w = w_out.shape[0]
    dv = v_w // heads
    dk = dv // 2
    qk_w = heads * dk
    g0 = 2 * qk_w + 2 * v_w
    n_g = w_in.shape[1] - g0
    assert dk == 128
    l = h_lat.shape[0]
    wb = w_in.astype(_BF16)
    w_g = jnp.pad(w_in[:, g0:], ((0, 0), (0, 128 - n_g)))
    sc_l, sh_l = mod_l
    sc_c, sh_c = mod_c
    qk_l = _mm(h_lat, wb[:, :2 * qk_w], scale=sc_l, shift=sh_l, name="ml_proj_qk")
    v_l = _mm(h_lat, wb[:, 2 * qk_w:2 * qk_w + v_w], scale=sc_l, shift=sh_l, out_dtype=_BF16, name="ml_proj_v")
    o_l = _mm(h_lat, wb[:, 2 * qk_w + v_w:g0], scale=sc_l, shift=sh_l, name="ml_proj_o")
    g_l = _ml_gates(_mm(h_lat, w_g, scale=sc_l, shift=sh_l, passes=3, name="ml_proj_g"), b_gates)
    k_c_raw = _mm(h_ctx, wb[:, qk_w:2 * qk_w], scale=sc_c, shift=sh_c, name="ml_proj_kc")
    v_c = _mm(h_ctx, wb[:, 2 * qk_w:2 * qk_w + v_w], scale=sc_c, shift=sh_c, out_dtype=_BF16, name="ml_proj_vc")
    g_c = _ml_gates(_mm(h_ctx, w_g, scale=sc_c, shift=sh_c, passes=3, name="ml_proj_gc"), b_gates)
    k_scale = dk ** -0.5
    q_l, k_l = _conv3(qk_l, conv_w, conv_b, n_out=2, out_dtype=_BF16, act=True, rope=_rope_tables(l),
                      out_scales=(1.0, k_scale))
    (k_c,) = _conv3(k_c_raw, conv_w[:, qk_w:], conv_b[qk_w:], n_out=1, out_dtype=_BF16, act=True,
                    out_scales=(k_scale,))
    init = (jnp.zeros((heads, dk, dv), _F32), jnp.zeros((heads, 1, dk), _F32), jnp.full((heads, 1, 128), ML_M0, _F32))
    scan = functools.partial(_ml_scan, dk=dk, dv=dv)
    _, st_f = scan(None, k_c, v_c, g_c, init, reverse=False)
    _, st_b = scan(None, k_c, v_c, g_c, init, reverse=True)
    hl_f, _ = scan(q_l, k_l, v_l, g_l, st_f, reverse=False)
    hl_b, _ = scan(q_l, k_l, v_l, g_l, st_b, reverse=True)
    merged = _ml_merge(hl_f, hl_b, o_l, norm_w, dv)
    return _mm(merged, w_out.astype(_BF16), name="ml_out")


_SUBLANES = 8


def _sort_network(n):
    pairs = []
    p = 1
    while p < n:
        k = p
        while k >= 1:
            for j in range(k % p, n - k, 2 * k):
                for i in range(min(k, n - j - k)):
                    if (i + j) // (2 * p) == (i + j + k) // (2 * p):
                        pairs.append((i + j, i + j + k))
            k //= 2
        p *= 2
    return pairs


def _top_desc(x, count):
    n = x.shape[0] // _SUBLANES
    tiles = [x[j * _SUBLANES:(j + 1) * _SUBLANES, :] for j in range(n)]
    for a, b in _sort_network(n):
        tiles[a], tiles[b] = jnp.maximum(tiles[a], tiles[b]), jnp.minimum(tiles[a], tiles[b])
    slot = lax.broadcasted_iota(jnp.int32, tiles[0].shape, 0).astype(_F32)
    vals = []
    for it in range(count):
        m = jnp.max(tiles[0], axis=0, keepdims=True)
        vals.append(m)
        remaining = count - it - 1
        if remaining == 0:
            break
        first = jnp.min(jnp.where(tiles[0] == m, slot, float(_SUBLANES)), axis=0, keepdims=True)
        hit = slot == first
        for j in range(min(remaining, n)):
            below = tiles[j + 1] if j + 1 < n else -jnp.inf
            tiles[j] = jnp.where(hit, below, tiles[j])
    return vals


def _peer_route_kernel(q_ref, k1_ref, k2_ref, e1_ref, e2_ref, th_ref):
    half = k1_ref.shape[1]
    q = q_ref[...]
    s1 = _dot(k1_ref[...], q[:, :half], 3, dims=_NT)
    s2 = _dot(k2_ref[...], q[:, half:], 3, dims=_NT)
    n_top = PEER_TOPK + 1
    v1 = _top_desc(s1, n_top)
    v2 = _top_desc(s2, n_top)
    cands = [v1[a] + v2[b] for a in range(n_top) for b in range(n_top) if (a + 1) * (b + 1) <= n_top]
    pad = 64 - len(cands)
    cand = jnp.concatenate(cands + [jnp.full_like(cands[0], -jnp.inf)] * pad, axis=0)
    top = _top_desc(cand, n_top)
    m = top[0]
    z = top[0] * 0.0
    for j in range(PEER_TOPK):
        z = z + jnp.exp(top[j] - m)
    tau = 0.5 * (top[PEER_TOPK - 1] + top[PEER_TOPK])
    inv_z = 1.0 / z
    e1_ref[0] = jnp.exp(s1 - v1[0])
    e2_ref[0] = jnp.exp(s2 - v2[0]) * inv_z
    th_ref[0] = jnp.exp(tau - m) * inv_z


def _rowwise_fp8(a):
    amax = jnp.max(jnp.abs(a), axis=1, keepdims=True)
    scale = jnp.maximum(amax, 1e-30) / _FP8_MAX
    return jnp.clip(a / scale, -_FP8_MAX, _FP8_MAX).astype(_FP8), scale


def _peer_dense_kernel(x_ref, sx_ref, u_ref, su_ref, vt_ref, e1_ref, e2_ref, th_ref, o_ref, act_ref, a_ref, *,
                       heads, nkeys):
    @pl.when(pl.program_id(1) == 0)
    def _():
        o_ref[...] = jnp.zeros_like(o_ref)

    n_sub = u_ref.shape[0] // nkeys
    n_part = 2
    part = u_ref.shape[0] // n_part
    lanes = 128

    def activations(pt):
        rows = slice(pt * part, (pt + 1) * part)
        act_ref[rows, :] = lax.dot_general(u_ref[rows, :], x_ref[...], _NT, preferred_element_type=_F32)

    def gated(i):
        rows = slice(i * nkeys, (i + 1) * nkeys)
        for c in range(x_ref.shape[0] // lanes):
            cols = slice(c * lanes, (c + 1) * lanes)
            gate = None
            for h in range(heads):
                p = e1_ref[h, i:i + 1, cols] * e2_ref[h, :, cols]
                sel = jnp.where(p >= th_ref[h, :, cols], p, 0.0)
                gate = sel if gate is None else gate + sel
            act = act_ref[rows, cols] * su_ref[rows, :] * sx_ref[:, cols]
            gelu = 0.5 * act * (1.0 + lax.erf(act * (2.0 ** -0.5)))
            a_ref[rows, cols] = (gate * gelu).astype(a_ref.dtype)

    activations(0)
    for pt in range(n_part):
        if pt + 1 < n_part:
            activations(pt + 1)
        for i in range(pt * n_sub // n_part, (pt + 1) * n_sub // n_part):
            gated(i)
        rows = slice(pt * part, (pt + 1) * part)
        o_ref[...] += jnp.dot(vt_ref[:, rows], a_ref[rows, :], preferred_element_type=_F32)


def _peer(x, wq, k1, k2, u_q, u_scale, vt_bf):
    t, d = x.shape
    heads, nkeys = PEER_HEADS, PEER_NKEYS
    qdim = wq.shape[1] // heads
    q = _mm(x, wq.astype(_BF16), name="peer_q")
    tt = _tile(t, 512, 128)
    e_shape = jax.ShapeDtypeStruct((heads, nkeys, t), _F32)
    e_blk = pl.BlockSpec((1, nkeys, tt), lambda i, h: (h, 0, i))
    th_blk = pl.BlockSpec((1, 1, tt), lambda i, h: (h, 0, i))
    e1, e2, th = pl.pallas_call(
        _peer_route_kernel,
        grid=(t // tt, heads),
        in_specs=[pl.BlockSpec((tt, qdim), lambda i, h: (i, h)),
                  pl.BlockSpec((nkeys, qdim // 2), lambda i, h: (0, 0)),
                  pl.BlockSpec((nkeys, qdim // 2), lambda i, h: (0, 0))],
        out_specs=[e_blk, e_blk, th_blk],
        out_shape=[e_shape, e_shape, jax.ShapeDtypeStruct((heads, 1, t), _F32)],
        compiler_params=_cp("parallel", "parallel"),
        name="peer_route",
    )(q, k1, k2)
    tb = _tile(t, 512, 128)
    n_sub = 8
    eb = n_sub * nkeys
    x_q, x_scale = _rowwise_fp8(x)
    return pl.pallas_call(
        functools.partial(_peer_dense_kernel, heads=heads, nkeys=nkeys),
        grid=(t // tb, nkeys // n_sub),
        in_specs=[pl.BlockSpec((tb, d), lambda i, e: (i, 0)),
                  pl.BlockSpec((1, tb), lambda i, e: (0, i)),
                  pl.BlockSpec((eb, d), lambda i, e: (e, 0)),
                  pl.BlockSpec((eb, 128), lambda i, e: (e, 0)),
                  pl.BlockSpec((d, eb), lambda i, e: (0, e)),
                  pl.BlockSpec((heads, n_sub, tb), lambda i, e: (0, e, i)),
                  pl.BlockSpec((heads, nkeys, tb), lambda i, e: (0, 0, i)),
                  pl.BlockSpec((heads, 1, tb), lambda i, e: (0, 0, i))],
        out_specs=pl.BlockSpec((d, tb), lambda i, e: (0, i)),
        out_shape=jax.ShapeDtypeStruct((d, t), _F32),
        scratch_shapes=[pltpu.VMEM((eb, tb), _F32), pltpu.VMEM((eb, tb), _BF16)],
        compiler_params=_cp("parallel", "arbitrary"),
        name="peer_dense",
    )(x_q, x_scale.reshape(1, t), u_q, u_scale, vt_bf, e1, e2, th)


def _hyena_na_mixer(h_lat, h_ctx, mod_l, mod_c, w_in, conv_w, conv_b, fparams, skip, rpb, w_out):
    split = conv_w.shape[1]
    wb = w_in.astype(_BF16)
    w_hy, w_na = wb[:, :split], wb[:, split:]
    outs = []
    qkvs = []
    for hh, (sc, sh) in ((h_lat, mod_l), (h_ctx, mod_c)):
        p_hy = _mm(hh, w_hy, scale=sc, shift=sh, name="hy_proj")
        qkvs.append(_mm(hh, w_na, scale=sc, shift=sh, out_dtype=_BF16, name="na_proj"))
        outs.append(_hyena(p_hy, conv_w, conv_b, fparams, skip))
    na_lat = _natten(qkvs[0], qkvs[1], rpb)
    na_ctx = _ctx_attn(qkvs[1])
    wo = w_out.astype(_BF16)
    y_lat = _mm(jnp.concatenate([outs[0], na_lat], axis=-1), wo, name="ab_out")
    y_ctx = _mm(jnp.concatenate([outs[1], na_ctx], axis=-1), wo, name="ab_out")
    return y_lat, y_ctx


def kernel(x, c, ctx, c_ctx, ada_w, ada_b, ln_w, ln_b, ab_w_in, hy_conv_w, hy_conv_b, hy_fw1, hy_fb1, hy_fw2, hy_fb2, hy_fw3, hy_fb3, hy_fw4, hy_skip, na_rpb, ab_w_out, ml_w_in, ml_b_gates, ml_conv_w, ml_conv_b, ml_norm_w, ml_w_out, peer_wq, peer_k1, peer_k2, peer_u, peer_v):
    bsz, seq_len, d = x.shape
    depth = ada_w.shape[0]
    assert bsz == 1 and depth == 2, "one even (Hyena/attention) layer followed by one last odd (mLSTM) layer"
    alpha = (2 * depth) ** 0.25
    h_lat, h_ctx = x[0], ctx[0]
    cvec = jnp.zeros((8, d), _F32).at[0].set(c[0]).at[1].set(c_ctx)
    row = lambda a: a.reshape(1, d)
    for layer in range(depth):
        last = layer == depth - 1
        mod = _mm(cvec, ada_w[layer], pre="silu", bias=ada_b[layer].reshape(1, 6 * d), passes=3, name="ada_mod")
        sh_l, sc_l, g_l, shf_l, scf_l, gf_l = [mod[0:1, i * d:(i + 1) * d] for i in range(6)]
        sh_c, sc_c, g_c, shf_c, scf_c, gf_c = [mod[1:2, i * d:(i + 1) * d] for i in range(6)]
        if layer % 2 == 0:
            fparams = (hy_fw1[0], hy_fb1[0], hy_fw2[0], hy_fb2[0], hy_fw3[0], hy_fb3[0], hy_fw4[0])
            y_lat, y_ctx = _hyena_na_mixer(h_lat, h_ctx, (sc_l, sh_l), (sc_c, sh_c), ab_w_in[0], hy_conv_w[0],
                                           hy_conv_b[0], fparams, hy_skip[0], na_rpb[0], ab_w_out[0])
        else:
            y_lat = _mlstm_mixer(h_lat, h_ctx, (sc_l, sh_l), (sc_c, sh_c), ml_w_in[0], ml_b_gates[0], ml_conv_w[0],
                                 ml_conv_b[0], ml_norm_w[0], ml_w_out[0])
            y_ctx = None
        lw0, lb0, lw1, lb1 = row(ln_w[layer, 0]), row(ln_b[layer, 0]), row(ln_w[layer, 1]), row(ln_b[layer, 1])
        u_q, u_scale = _rowwise_fp8(peer_u[layer])
        u_scale = jnp.broadcast_to(u_scale, (u_scale.shape[0], 128))
        vt_bf = peer_v[layer].astype(_BF16).T
        peer = lambda v: _peer(v, peer_wq[layer], peer_k1[layer], peer_k2[layer], u_q, u_scale, vt_bf)
        h_lat, v_lat = _ln_res(h_lat, y_lat, g_l, lw0, lb0, scf_l, shf_l, alpha=alpha, with_mod=True)
        h_lat, _ = _ln_res(h_lat, peer(v_lat), gf_l, lw1, lb1, scf_l, shf_l, alpha=alpha, with_mod=False,
                           y_transposed=True)
        if not last:
            h_ctx, v_ctx = _ln_res(h_ctx, y_ctx, g_c, lw0, lb0, scf_c, shf_c, alpha=alpha, with_mod=True)
            h_ctx, _ = _ln_res(h_ctx, peer(v_ctx), gf_c, lw1, lb1, scf_c, shf_c, alpha=alpha, with_mod=False,
                               y_transposed=True)
    return h_lat[None]
```

```python
import functools
import math

import numpy as np
import jax
import jax.numpy as jnp
from jax import lax
from jax.experimental import pallas as pl
from jax.experimental.pallas import tpu as pltpu

_F32 = jnp.float32
_BF16 = jnp.bfloat16
_FP8 = jnp.float8_e4m3fn
_FP8_MAX = 448.0

GRID_W = 64
HY_ORDER = 2
HY_EMB = 33
HY_DECAY_TARGET = 1e-2
HY_FAST_PCT = 0.3
HY_SLOW_PCT = 1.5
NA_HEADS = 8
NA_WIN_ROWS = 8
NA_WIN_COLS = 16
ML_HEADS = 8
ML_TILE = 256
ML_M0 = -1e30
ROPE_BASE = 10000.0
PEER_HEADS = 8
PEER_NKEYS = 128
PEER_TOPK = 16
LN_EPS = 1e-6

_VMEM_LIMIT = 52 * 1024 * 1024
_NEG = -1e30
_FFT_DT = jnp.bfloat16

_NT = (((1,), (1,)), ((), ()))
_TN = (((0,), (0,)), ((), ()))


def _cp(*sem):
    return pltpu.CompilerParams(dimension_semantics=sem, vmem_limit_bytes=_VMEM_LIMIT)


def _tile(n, cap, align):
    t = (min(cap, n) // align) * align
    while t >= align:
        if n % t == 0:
            return t
        t -= align
    return n


def _split(x):
    hi = x.astype(_BF16)
    lo = (x.astype(_F32) - hi.astype(_F32)).astype(_BF16)
    return hi, lo


def _dot(a, b, passes=1, dims=None):
    dn = dims or (((a.ndim - 1,), (0,)), ((), ()))
    f = lambda x, y: lax.dot_general(x, y, dn, preferred_element_type=_F32)
    if passes == 1:
        return f(a.astype(_BF16), b.astype(_BF16))
    ah, al = _split(a)
    bh, bl = _split(b)
    return f(ah, bh) + (f(ah, bl) + f(al, bh))


def _sigmoid(x):
    return 1.0 / (1.0 + jnp.exp(-x))


def _mm_kernel(*refs, pre, has_bias, passes):
    it = iter(refs)
    a_ref = next(it)
    if pre == "mod":
        sc_ref, sh_ref = next(it), next(it)
    b_ref = next(it)
    bias_ref = next(it) if has_bias else None
    o_ref = next(it)
    a = a_ref[...]
    if pre == "mod":
        a = a.astype(_F32) * (1.0 + sc_ref[...]) + sh_ref[...]
    elif pre == "silu":
        a = a * _sigmoid(a)
    acc = _dot(a, b_ref[...], passes)
    if has_bias:
        acc = acc + bias_ref[...]
    o_ref[...] = acc.astype(o_ref.dtype)


def _mm(a, b, *, scale=None, shift=None, pre=None, bias=None, out_dtype=_F32, passes=1, name="mm"):
    m, k = a.shape
    n = b.shape[1]
    if scale is not None:
        pre = "mod"
    tile_bytes = 4 * 1024 * 1024
    tm = _tile(m, max(8, min(1024, 2 * tile_bytes // (k * a.dtype.itemsize))), 8)
    tn = _tile(n, max(128, min(8192, tile_bytes // (k * b.dtype.itemsize), tile_bytes // (tm * 4))), 128)
    in_specs = [pl.BlockSpec((tm, k), lambda i, j: (i, 0))]
    args = [a]
    if pre == "mod":
        in_specs += [pl.BlockSpec((1, k), lambda i, j: (0, 0))] * 2
        args += [scale, shift]
    in_specs.append(pl.BlockSpec((k, tn), lambda i, j: (0, j)))
    args.append(b)
    if bias is not None:
        in_specs.append(pl.BlockSpec((1, tn), lambda i, j: (0, j)))
        args.append(bias)
    return pl.pallas_call(
        functools.partial(_mm_kernel, pre=pre, has_bias=bias is not None, passes=passes),
        grid=(m // tm, n // tn),
        in_specs=in_specs,
        out_specs=pl.BlockSpec((tm, tn), lambda i, j: (i, j)),
        out_shape=jax.ShapeDtypeStruct((m, n), out_dtype),
        compiler_params=_cp("parallel", "parallel"),
        name=name,
    )(*args)


def _ln_res_kernel(h_ref, y_ref, g_ref, w_ref, b_ref, sc_ref, sh_ref, *o_refs, alpha, with_mod, y_transposed):
    y = y_ref[...].T if y_transposed else y_ref[...]
    z = alpha * h_ref[...] + g_ref[...] * y
    mu = jnp.mean(z, axis=-1, keepdims=True)
    zc = z - mu
    var = jnp.mean(zc * zc, axis=-1, keepdims=True)
    hn = zc * lax.rsqrt(var + LN_EPS) * w_ref[...] + b_ref[...]
    o_refs[0][...] = hn
    if with_mod:
        o_refs[1][...] = hn * (1.0 + sc_ref[...]) + sh_ref[...]


def _ln_res(h, y, g, w, b, sc, sh, *, alpha, with_mod, y_transposed=False):
    m, d = h.shape
    tr = _tile(m, 256, 128 if y_transposed else 8)
    row = pl.BlockSpec((tr, d), lambda i: (i, 0))
    y_spec = pl.BlockSpec((d, tr), lambda i: (0, i)) if y_transposed else row
    vec = pl.BlockSpec((1, d), lambda i: (0, 0))
    n_out = 2 if with_mod else 1
    out = pl.pallas_call(
        functools.partial(_ln_res_kernel, alpha=alpha, with_mod=with_mod, y_transposed=y_transposed),
        grid=(m // tr,),
        in_specs=[row, y_spec, vec, vec, vec, vec, vec],
        out_specs=[row] * n_out,
        out_shape=[jax.ShapeDtypeStruct((m, d), _F32)] * n_out,
        compiler_params=_cp("parallel"),
        name="ln_res",
    )(h, y, g, w, b, sc, sh)
    return out if with_mod else (out[0], None)


def _conv3_kernel(*refs, n_out, act, rope, out_scales):
    it = iter(refs)
    xp_ref, x_ref, xn_ref, w_ref, b_ref = (next(it) for _ in range(5))
    if rope:
        cos_ref, sin_ref = next(it), next(it)
    o_refs = [next(it) for _ in range(n_out)]
    i = pl.program_id(0)
    x = x_ref[...]
    tr = x.shape[0]
    rows = lax.broadcasted_iota(jnp.int32, x.shape, 0)
    prev_row = jnp.where(i > 0, xp_ref[7:8, :], 0.0)
    next_row = jnp.where(i < pl.num_programs(0) - 1, xn_ref[0:1, :], 0.0)
    x_m1 = jnp.where(rows == 0, prev_row, pltpu.roll(x, 1, axis=0))
    x_p1 = jnp.where(rows == tr - 1, next_row, pltpu.roll(x, tr - 1, axis=0))
    y = b_ref[...] + x_m1 * w_ref[0:1, :] + x * w_ref[1:2, :] + x_p1 * w_ref[2:3, :]
    if act:
        y = y * _sigmoid(y)
    wo = y.shape[1] // n_out
    for o in range(n_out):
        yo = y[:, o * wo:(o + 1) * wo]
        if rope:
            cos, sin = cos_ref[...], sin_ref[...]
            lane = lax.broadcasted_iota(jnp.int32, cos.shape, 1)
            first = (lane % 64) < 32
            parts = []
            for hh in range(wo // 128):
                yh = yo[:, hh * 128:(hh + 1) * 128]
                sw = jnp.where(first, pltpu.roll(yh, 96, axis=1), pltpu.roll(yh, 32, axis=1))
                parts.append(yh * cos + sw * sin)
            yo = jnp.concatenate(parts, axis=1)
        if out_scales[o] != 1.0:
            yo = yo * out_scales[o]
        o_refs[o][...] = yo.astype(o_refs[o].dtype)


def _conv3(x, w, b, *, n_out, out_dtype, act=False, rope=None, out_scales=None):
    l, cx = x.shape
    tr = _tile(l, 256, 8)
    nb8 = l // 8
    out_scales = out_scales or (1.0,) * n_out
    in_specs = [
        pl.BlockSpec((8, cx), lambda i: (jnp.maximum(i * (tr // 8) - 1, 0), 0)),
        pl.BlockSpec((tr, cx), lambda i: (i, 0)),
        pl.BlockSpec((8, cx), lambda i: (jnp.minimum((i + 1) * (tr // 8), nb8 - 1), 0)),
        pl.BlockSpec((3, cx), lambda i: (0, 0)),
        pl.BlockSpec((1, cx), lambda i: (0, 0)),
    ]
    args = [x, x, x, w, b.reshape(1, cx)]
    if rope is not None:
        in_specs += [pl.BlockSpec((tr, 128), lambda i: (i, 0))] * 2
        args += list(rope)
    wo = cx // n_out
    return pl.pallas_call(
        functools.partial(_conv3_kernel, n_out=n_out, act=act, rope=rope is not None, out_scales=out_scales),
        grid=(l // tr,),
        in_specs=in_specs,
        out_specs=[pl.BlockSpec((tr, wo), lambda i: (i, 0))] * n_out,
        out_shape=[jax.ShapeDtypeStruct((l, wo), out_dtype)] * n_out,
        compiler_params=_cp("parallel"),
        name="conv3",
    )(*args)


def _hy_filter_kernel(f_ref, t_ref, w1_ref, b1_ref, w2_ref, b2_ref, w3_ref, b3_ref, w4_ref, d_ref, o_ref):
    h = jnp.sin(_dot(f_ref[...], w1_ref[...], 3) + b1_ref[...])
    h = jnp.sin(_dot(h, w2_ref[...], 3) + b2_ref[...])
    h = jnp.sin(_dot(h, w3_ref[...], 3) + b3_ref[...])
    taps = _dot(h, w4_ref[...], 3)
    tv = t_ref[...]
    window = jnp.exp(-tv[:, 0:1] * d_ref[...]) * tv[:, 1:2]
    c = window.shape[1]
    for g in range(taps.shape[1] // c):
        o_ref[:, g * c:(g + 1) * c] = (taps[:, g * c:(g + 1) * c] * window).astype(o_ref.dtype)


def _hy_taps(seq_len, l_pad, fw1, fb1, fw2, fb2, fw3, fb3, fw4, width):
    hid = fw1.shape[1]
    n = np.arange(2 * l_pad)
    off = np.where(n < l_pad, n, 2 * l_pad - n)
    exists = np.where(n < l_pad, off < seq_len, (off >= 1) & (off < seq_len))
    pos = jnp.asarray(np.where(exists, off, 0), _F32)
    t = pos / (seq_len - 1)
    omega = 2.0 * math.pi * pos / seq_len
    t = jnp.stack([t, jnp.asarray(exists, _F32)], axis=-1)
    bands = (HY_EMB - 1) // 2
    freqs = jnp.linspace(1e-4, bands - 1, bands, dtype=_F32)
    ang = omega[:, None] * freqs[None, :]
    kpad = 64
    feats = jnp.concatenate([t[:, 0:1], jnp.cos(ang), -jnp.sin(ang), jnp.zeros((2 * l_pad, kpad - HY_EMB), _F32)],
                            axis=-1)
    w1 = jnp.concatenate([fw1, jnp.zeros((kpad - HY_EMB, hid), _F32)], axis=0)
    max_decay = math.log(HY_DECAY_TARGET) / HY_FAST_PCT
    min_decay = math.log(HY_DECAY_TARGET) / HY_SLOW_PCT
    deltas = jnp.abs(jnp.linspace(min_decay, max_decay, width, dtype=_F32))[None, :]
    n_cols = fw4.shape[1] // 2
    tl = _tile(l_pad, 256, 8)
    per_dir = l_pad // tl
    full = lambda shape: pl.BlockSpec(shape, lambda i: (0,) * len(shape))
    return pl.pallas_call(
        _hy_filter_kernel,
        grid=(2 * per_dir,),
        in_specs=[pl.BlockSpec((tl, kpad), lambda i: (i, 0)), pl.BlockSpec((tl, 2), lambda i: (i, 0)),
                  full((kpad, hid)), full((1, hid)), full((hid, hid)), full((1, hid)), full((hid, hid)), full((1, hid)),
                  pl.BlockSpec((hid, n_cols), lambda i: (0, i // per_dir)), full((1, width))],
        out_specs=pl.BlockSpec((tl, n_cols), lambda i: (i, 0)),
        out_shape=jax.ShapeDtypeStruct((2 * l_pad, n_cols), _BF16),
        compiler_params=_cp("parallel"),
        name="hy_filter",
    )(feats, t, w1, fb1.reshape(1, hid), fw2, fb2.reshape(1, hid), fw3, fb3.reshape(1, hid), fw4, deltas)


def _fft_consts(n1, n2):
    n = n1 * n2
    k1 = np.arange(n1)[:, None]
    j1 = np.arange(n1 // 2)[None, :]
    ang1 = 2.0 * np.pi * k1 * j1 / n1
    f1_fwd = np.concatenate([np.cos(ang1), -np.sin(ang1)], axis=0)
    f1_inv = np.concatenate([np.cos(ang1).T, -np.sin(ang1).T], axis=1) / n
    k2 = np.arange(n2)[:, None]
    j2 = np.arange(n2)[None, :]
    ang2 = 2.0 * np.pi * k2 * j2 / n2
    f2 = np.stack([np.cos(ang2), -np.sin(ang2)])
    angt = 2.0 * np.pi * np.arange(n1)[:, None] * np.arange(n2)[None, :] / n
    tw = np.stack([np.cos(angt), -np.sin(angt)], axis=1)
    angf = 2.0 * np.pi * k1 * np.arange(n1)[None, :] / n1
    f1_full = np.concatenate([np.cos(angf), -np.sin(angf)], axis=0)
    as32 = lambda a: jnp.asarray(a, dtype=_F32)
    return as32(f1_fwd), as32(f1_inv), as32(f2), as32(tw), as32(np.transpose(tw, (0, 2, 1))), as32(f1_full)


def _cmm(f_stack, xr, xi):
    n = f_stack.shape[0] // 2
    p1 = jnp.dot(f_stack, xr.astype(_BF16), preferred_element_type=_F32)
    p2 = jnp.dot(f_stack, xi.astype(_BF16), preferred_element_type=_F32)
    return p1[:n] - p2[n:], p2[:n] + p1[n:]


def _twiddled_dft(f2_ref, tw_ref):
    f2r, f2i = f2_ref[0], f2_ref[1]
    tr, ti = tw_ref[0, 0:1, :], tw_ref[0, 1:2, :]
    return jnp.concatenate([f2r * tr - f2i * ti, f2r * ti + f2i * tr], axis=0).astype(_BF16)


def _fft_spec_kernel(a_ref, f2_ref, tw_ref, o_ref):
    fw = _twiddled_dft(f2_ref, tw_ref)
    hr, hi = _cmm(fw, a_ref[0, 0], a_ref[1, 0])
    o_ref[0, 0, 0] = hr.astype(o_ref.dtype)
    o_ref[0, 1, 0] = hi.astype(o_ref.dtype)


def _fft_mid_kernel(a_ref, h_ref, f2_ref, tw_ref, twc_ref, o_ref):
    fw = _twiddled_dft(f2_ref, tw_ref)
    br, bi = _cmm(fw, a_ref[0, 0], a_ref[1, 0])
    hr, hi = h_ref[0, 0, 0].astype(_F32), h_ref[0, 1, 0].astype(_F32)
    yr = (br * hr - bi * hi).astype(_BF16)
    yi = (br * hi + bi * hr).astype(_BF16)
    f2r, f2i = f2_ref[0], f2_ref[1]
    tcr, tci = twc_ref[0, :, 0:1], twc_ref[0, :, 1:2]
    g = jnp.concatenate([tcr * f2r - tci * f2i, -(tcr * f2i + tci * f2r)], axis=0).astype(_BF16)
    zr, zi = _cmm(g, yr, yi)
    o_ref[0, 0] = zr.astype(o_ref.dtype)
    o_ref[1, 0] = zi.astype(o_ref.dtype)


def _fft_shape(l_pad):
    n2 = 256 if l_pad >= 8192 else 64
    n1 = 2 * l_pad // n2
    return n1, n2


def _hy_spectra(taps, width, consts, n1, n2):
    f1_fwd, _, f2, tw, _, f1_full = consts
    half = taps.shape[1]
    a = _mm(f1_full, taps.reshape(n1, n2 * half), out_dtype=_FFT_DT, name="fft_a_filt")
    a = a.reshape(2, n1, n2, half)
    return pl.pallas_call(
        _fft_spec_kernel,
        grid=(n1, HY_ORDER),
        in_specs=[pl.BlockSpec((2, 1, n2, width), lambda k, o: (0, k, 0, o)),
                  pl.BlockSpec((2, n2, n2), lambda k, o: (0, 0, 0)),
                  pl.BlockSpec((1, 2, n2), lambda k, o: (k, 0, 0))],
        out_specs=pl.BlockSpec((1, 2, 1, n2, width), lambda k, o: (o, 0, k, 0, 0)),
        out_shape=jax.ShapeDtypeStruct((HY_ORDER, 2, n1, n2, width), _FFT_DT),
        compiler_params=_cp("parallel", "parallel"),
        name="fft_spec",
    )(a, f2, tw)


def _long_conv(z, spec, order, consts, n1, n2):
    f1_fwd, f1_inv, f2, tw, twc, _ = consts
    l_pad, c = z.shape
    a = _mm(f1_fwd, z.reshape(n1 // 2, n2 * c), out_dtype=_FFT_DT, name="fft_a")
    a = a.reshape(2, n1, n2, c)
    blk = pl.BlockSpec((2, 1, n2, c), lambda k: (0, k, 0, 0))
    mid = pl.pallas_call(
        _fft_mid_kernel,
        grid=(n1,),
        in_specs=[blk, pl.BlockSpec((1, 2, 1, n2, c), lambda k: (order, 0, k, 0, 0)),
                  pl.BlockSpec((2, n2, n2), lambda k: (0, 0, 0)),
                  pl.BlockSpec((1, 2, n2), lambda k: (k, 0, 0)), pl.BlockSpec((1, n2, 2), lambda k: (k, 0, 0))],
        out_specs=blk,
        out_shape=jax.ShapeDtypeStruct((2, n1, n2, c), _FFT_DT),
        compiler_params=_cp("parallel"),
        name="fft_mid",
    )(a, spec, f2, tw, twc)
    y = _mm(f1_inv, mid.reshape(2 * n1, n2 * c), name="fft_c")
    return y.reshape(l_pad, c)


def _hy_gate_kernel(y_ref, z_ref, s_ref, g_ref, o_ref):
    o_ref[...] = (g_ref[...] * (y_ref[...] + z_ref[...] * s_ref[...])).astype(o_ref.dtype)


def _hy_gate(y, z, skip, gate, out_dtype):
    l, c = y.shape
    tr = _tile(l, 512, 8)
    row = pl.BlockSpec((tr, c), lambda i: (i, 0))
    return pl.pallas_call(
        _hy_gate_kernel,
        grid=(l // tr,),
        in_specs=[row, row, pl.BlockSpec((1, c), lambda i: (0, 0)), row],
        out_specs=row,
        out_shape=jax.ShapeDtypeStruct((l, c), out_dtype),
        compiler_params=_cp("parallel"),
        name="hy_gate",
    )(y, z, skip.reshape(1, c), gate)


def _hyena(p, conv_w, conv_b, fparams, skip):
    l = p.shape[0]
    width = p.shape[1] // (HY_ORDER + 1)
    l_pad = max(l, 1024)
    n1, n2 = _fft_shape(l_pad)
    consts = _fft_consts(n1, n2)
    taps = _hy_taps(l, l_pad, *fparams, width)
    parts = _conv3(p, conv_w, conv_b, n_out=HY_ORDER + 1, out_dtype=_F32)
    if l_pad != l:
        parts = [jnp.pad(a, ((0, l_pad - l), (0, 0))) for a in parts]
    spec = _hy_spectra(taps, width, consts, n1, n2)
    z = parts[0]
    for o in range(HY_ORDER):
        y = _long_conv(z, spec, o, consts, n1, n2)
        z = _hy_gate(y, z, skip[o], parts[o + 1], _F32 if o + 1 < HY_ORDER else _BF16)
    return z[:l]


def _na_bias_table(rpb, w):
    kr, kc = NA_WIN_ROWS, NA_WIN_COLS
    cols = np.arange(w)
    start = np.clip(cols - kc // 2, 0, w - kc)
    kcol = np.arange(w)[None, :]
    in_win = (kcol >= start[:, None]) & (kcol < start[:, None] + kc)
    rp = jnp.pad(rpb, ((0, 0), (0, 0), (w - 1, w)))
    by_q = jnp.stack([rp[:, :, kc - 1 + w - 1 - q:kc - 1 + 2 * w - 1 - q] for q in range(w)], axis=2)
    by_q = jnp.where(in_win[None, None], by_q, _NEG)
    t = jnp.stack([jnp.concatenate([by_q[:, i - d + kr - 1] for i in range(kr)], axis=-1) for d in range(kr)], axis=1)
    return t


def _natten_kernel(q_ref, k_ref, v_ref, kc_ref, vc_ref, bias_ref, o_ref, *, rows, w, scale):
    qb = pl.program_id(1)
    kr = NA_WIN_ROWS
    kc, vc = kc_ref[...], vc_ref[...]
    for a in range(kr):
        r = qb * kr + a
        r0 = jnp.clip(r - kr // 2, 0, rows - kr)
        d = r - r0
        start = pl.multiple_of(r0 * w, w)
        kwin = k_ref[pl.ds(start, kr * w), :]
        vwin = v_ref[pl.ds(start, kr * w), :]
        qa = q_ref[a * w:(a + 1) * w, :]
        s_lat = _dot(qa, kwin, dims=_NT) * scale + bias_ref[0, d]
        s_ctx = _dot(qa, kc, dims=_NT) * scale
        m = jnp.maximum(jnp.max(s_lat, axis=-1, keepdims=True), jnp.max(s_ctx, axis=-1, keepdims=True))
        p_lat = jnp.exp(s_lat - m)
        p_ctx = jnp.exp(s_ctx - m)
        den = jnp.sum(p_lat, axis=-1, keepdims=True) + jnp.sum(p_ctx, axis=-1, keepdims=True)
        o = _dot(p_lat, vwin) + _dot(p_ctx, vc)
        o_ref[a * w:(a + 1) * w, :] = (o / den).astype(o_ref.dtype)


def _natten(qkv, kv_ctx, rpb):
    l = qkv.shape[0]
    h = NA_HEADS
    hd = qkv.shape[1] // (3 * h)
    w = GRID_W
    rows = l // w
    kr = NA_WIN_ROWS
    assert rows % kr == 0 and rows >= kr and hd % 128 == 0
    lc = kv_ctx.shape[0]
    bias = _na_bias_table(rpb, w)
    tq = kr * w
    return pl.pallas_call(
        functools.partial(_natten_kernel, rows=rows, w=w, scale=hd ** -0.5),
        grid=(h, rows // kr),
        in_specs=[pl.BlockSpec((tq, hd), lambda hh, qb: (qb, hh)),
                  pl.BlockSpec((l, hd), lambda hh, qb: (0, h + hh)),
                  pl.BlockSpec((l, hd), lambda hh, qb: (0, 2 * h + hh)),
                  pl.BlockSpec((lc, hd), lambda hh, qb: (0, h + hh)),
                  pl.BlockSpec((lc, hd), lambda hh, qb: (0, 2 * h + hh)),
                  pl.BlockSpec((1, kr, w, kr * w), lambda hh, qb: (hh, 0, 0, 0))],
        out_specs=pl.BlockSpec((tq, hd), lambda hh, qb: (qb, hh)),
        out_shape=jax.ShapeDtypeStruct((l, h * hd), _BF16),
        compiler_params=_cp("parallel", "parallel"),
        name="natten",
    )(qkv, qkv, qkv, kv_ctx, kv_ctx, bias)


def _ctx_attn_kernel(q_ref, k_ref, v_ref, o_ref, *, scale):
    s = _dot(q_ref[...], k_ref[...], dims=_NT) * scale
    p = jnp.exp(s - jnp.max(s, axis=-1, keepdims=True))
    den = jnp.sum(p, axis=-1, keepdims=True)
    o_ref[...] = (_dot(p, v_ref[...]) / den).astype(o_ref.dtype)


def _ctx_attn(qkv):
    lc = qkv.shape[0]
    h = NA_HEADS
    hd = qkv.shape[1] // (3 * h)
    blk = lambda off: pl.BlockSpec((lc, hd), lambda hh: (0, off + hh))
    return pl.pallas_call(
        functools.partial(_ctx_attn_kernel, scale=hd ** -0.5),
        grid=(h,),
        in_specs=[blk(0), blk(h), blk(2 * h)],
        out_specs=blk(0),
        out_shape=jax.ShapeDtypeStruct((lc, h * hd), _BF16),
        compiler_params=_cp("parallel"),
        name="ctx_attn",
    )(qkv, qkv, qkv)


def _ml_gates_kernel(g_ref, b_ref, o_ref, *, heads):
    g = g_ref[...] + b_ref[...]
    logsig = jnp.minimum(g, 0.0) - jnp.log(1.0 + jnp.exp(-jnp.abs(g)))
    lane = lax.broadcasted_iota(jnp.int32, g.shape, 1)
    is_forget = (lane // heads) % 2 == 1
    o_ref[...] = jnp.where(is_forget, logsig, g)


def _ml_gates(g_raw, b_gates):
    l, n = g_raw.shape
    tr = _tile(l, 1024, 8)
    bias = jnp.pad(b_gates, (0, n - b_gates.shape[0])).reshape(1, n)
    return pl.pallas_call(
        functools.partial(_ml_gates_kernel, heads=ML_HEADS),
        grid=(l // tr,),
        in_specs=[pl.BlockSpec((tr, n), lambda i: (i, 0)), pl.BlockSpec((1, n), lambda i: (0, 0))],
        out_specs=pl.BlockSpec((tr, n), lambda i: (i, 0)),
        out_shape=jax.ShapeDtypeStruct((l, n), _F32),
        compiler_params=_cp("parallel"),
        name="ml_gates",
    )(g_raw, bias)


def _ml_scan_kernel(*refs, heads, dk, dv, reverse, with_h):
    it = iter(refs)
    if with_h:
        q_ref = next(it)
    k_ref, v_ref, g_ref, gt_ref, c0_ref, n0_ref, m0_ref = (next(it) for _ in range(7))
    if with_h:
        h_ref = next(it)
    c_ref, n_ref, m_ref = next(it), next(it), next(it)

    @pl.when(pl.program_id(0) == 0)
    def _():
        c_ref[...] = c0_ref[...]
        n_ref[...] = n0_ref[...]
        m_ref[...] = m0_ref[...]

    t = k_ref.shape[0]
    ti = lax.broadcasted_iota(jnp.int32, (t, t), 0)
    si = lax.broadcasted_iota(jnp.int32, (t, t), 1)
    seen = (si >= ti) if reverse else (si <= ti)
    seen_t = (ti >= si) if reverse else (ti <= si)
    g = g_ref[...]
    gt = gt_ref[0]
    c0 = 2 * heads if reverse else 0
    for h in range(heads):
        i_col = g[:, c0 + h:c0 + h + 1]
        f_col = g[:, c0 + heads + h:c0 + heads + h + 1]
        i_row = gt[c0 + h:c0 + h + 1, :]
        f_row = gt[c0 + heads + h:c0 + heads + h + 1, :]
        b_col = jnp.sum(jnp.where(seen, f_row, 0.0), axis=1, keepdims=True)
        b_row = jnp.sum(jnp.where(seen_t, f_col, 0.0), axis=0, keepdims=True)
        b_end = jnp.sum(f_row, axis=1, keepdims=True)
        m_prev = m_ref[h][:, 0:1]
        w_log_row = b_end - b_row + i_row
        w_log_col = b_end - b_col + i_col
        m_new = jnp.maximum(b_end + m_prev, jnp.max(w_log_row, axis=1, keepdims=True))
        decay = jnp.exp(b_end + m_prev - m_new)
        w_col = jnp.exp(w_log_col - m_new)
        kh = k_ref[:, h * dk:(h + 1) * dk]
        vh = v_ref[:, h * dv:(h + 1) * dv]
        ct = c_ref[h]
        nv = n_ref[h]
        if with_h:
            qh = q_ref[:, h * dk:(h + 1) * dk]
            log_d = jnp.where(seen, b_col - b_row + i_row, -jnp.inf)
            m_in = b_col + m_prev
            m_t = jnp.maximum(m_in, jnp.max(log_d, axis=1, keepdims=True))
            sc = _dot(qh, kh, dims=_NT) * jnp.exp(log_d - m_t)
            g_in = jnp.exp(m_in - m_t)
            num = _dot(sc, vh) + _dot(qh, ct) * g_in
            den = jnp.sum(sc, axis=1, keepdims=True) + jnp.sum(qh.astype(_F32) * nv, axis=1, keepdims=True) * g_in
            h_ref[:, h * dv:(h + 1) * dv] = num / jnp.maximum(jnp.abs(den), jnp.exp(-m_t))
        vw = vh.astype(_F32) * w_col
        c_ref[h] = decay * ct + _dot(kh, vw, dims=_TN)
        n_ref[h] = decay * nv + jnp.sum(kh.astype(_F32) * w_col, axis=0, keepdims=True)
        m_ref[h] = jnp.broadcast_to(m_new, m_ref.shape[1:])


def _ml_scan(q, k, v, gates, state, *, reverse, dk, dv):
    l = k.shape[0]
    heads = ML_HEADS
    t = min(ML_TILE, l)
    assert l % t == 0
    nc = l // t
    with_h = q is not None
    gt = jnp.transpose(gates.reshape(nc, t, gates.shape[1]), (0, 2, 1))
    cidx = (lambda j: nc - 1 - j) if reverse else (lambda j: j)
    row = lambda width: pl.BlockSpec((t, width), lambda j: (cidx(j), 0))
    st_specs = [pl.BlockSpec((heads, dk, dv), lambda j: (0, 0, 0)), pl.BlockSpec((heads, 1, dk), lambda j: (0, 0, 0)),
                pl.BlockSpec((heads, 1, 128), lambda j: (0, 0, 0))]
    st_shapes = [jax.ShapeDtypeStruct((heads, dk, dv), _F32), jax.ShapeDtypeStruct((heads, 1, dk), _F32),
                 jax.ShapeDtypeStruct((heads, 1, 128), _F32)]
    in_specs = ([row(heads * dk)] if with_h else []) + [
        row(heads * dk), row(heads * dv), row(gates.shape[1]),
        pl.BlockSpec((1, gates.shape[1], t), lambda j: (cidx(j), 0, 0))] + st_specs
    args = ([q] if with_h else []) + [k, v, gates, gt, *state]
    out_specs = ([row(heads * dv)] if with_h else []) + st_specs
    out_shape = ([jax.ShapeDtypeStruct((l, heads * dv), _F32)] if with_h else []) + st_shapes
    out = pl.pallas_call(
        functools.partial(_ml_scan_kernel, heads=heads, dk=dk, dv=dv, reverse=reverse, with_h=with_h),
        grid=(nc,),
        in_specs=in_specs,
        out_specs=out_specs,
        out_shape=out_shape,
        compiler_params=_cp("arbitrary"),
        name="ml_scan",
    )(*args)
    return (out[0], tuple(out[1:])) if with_h else (None, tuple(out))


def _ml_merge_kernel(hf_ref, hb_ref, o_ref, w_ref, out_ref, *, heads, dv):
    for h in range(heads):
        sl = slice(h * dv, (h + 1) * dv)
        x = hf_ref[:, sl] + hb_ref[:, sl]
        mu = jnp.mean(x, axis=-1, keepdims=True)
        xc = x - mu
        var = jnp.mean(xc * xc, axis=-1, keepdims=True)
        hn = xc * lax.rsqrt(var + LN_EPS) * w_ref[:, sl]
        out_ref[:, sl] = (hn * _sigmoid(o_ref[:, sl])).astype(out_ref.dtype)


def _ml_merge(hf, hb, o, norm_w, dv):
    l, n = hf.shape
    tr = _tile(l, 256, 8)
    row = pl.BlockSpec((tr, n), lambda i: (i, 0))
    return pl.pallas_call(
        functools.partial(_ml_merge_kernel, heads=ML_HEADS, dv=dv),
        grid=(l // tr,),
        in_specs=[row, row, row, pl.BlockSpec((1, n), lambda i: (0, 0))],
        out_specs=row,
        out_shape=jax.ShapeDtypeStruct((l, n), _BF16),
        compiler_params=_cp("parallel"),
        name="ml_merge",
    )(hf, hb, o, norm_w.reshape(1, n))


def _rope_tables(l):
    quarter = 32
    pos = jnp.arange(l)
    inv = ROPE_BASE ** (-jnp.arange(quarter, dtype=_F32) / quarter)
    ang_r = (pos // GRID_W).astype(_F32)[:, None] * inv[None]
    ang_c = (pos % GRID_W).astype(_F32)[:, None] * inv[None]
    cos = jnp.concatenate([jnp.cos(ang_r)] * 2 + [jnp.cos(ang_c)] * 2, axis=-1)
    sin = jnp.concatenate([-jnp.sin(ang_r), jnp.sin(ang_r), -jnp.sin(ang_c), jnp.sin(ang_c)], axis=-1)
    return cos, sin


def _mlstm_mixer(h_lat, h_ctx, mod_l, mod_c, w_in, b_gates, conv_w, conv_b, norm_w, w_out):
    heads = ML_HEADS
    v_w = w_out.shape[0]
    dv = v_w // heads
    dk = dv // 2
    qk_w = heads * dk
    g0 = 2 * qk_w + 2 * v_w
    n_g = w_in.shape[1] - g0
    assert dk == 128
    l = h_lat.shape[0]
    wb = w_in.astype(_BF16)
    w_g = jnp.pad(w_in[:, g0:], ((0, 0), (0, 128 - n_g)))
    sc_l, sh_l = mod_l
    sc_c, sh_c = mod_c
    qk_l = _mm(h_lat, wb[:, :2 * qk_w], scale=sc_l, shift=sh_l, name="ml_proj_qk")
    v_l = _mm(h_lat, wb[:, 2 * qk_w:2 * qk_w + v_w], scale=sc_l, shift=sh_l, out_dtype=_BF16, name="ml_proj_v")
    o_l = _mm(h_lat, wb[:, 2 * qk_w + v_w:g0], scale=sc_l, shift=sh_l, name="ml_proj_o")
    g_l = _ml_gates(_mm(h_lat, w_g, scale=sc_l, shift=sh_l, passes=3, name="ml_proj_g"), b_gates)
    k_c_raw = _mm(h_ctx, wb[:, qk_w:2 * qk_w], scale=sc_c, shift=sh_c, name="ml_proj_kc")
    v_c = _mm(h_ctx, wb[:, 2 * qk_w:2 * qk_w + v_w], scale=sc_c, shift=sh_c, out_dtype=_BF16, name="ml_proj_vc")
    g_c = _ml_gates(_mm(h_ctx, w_g, scale=sc_c, shift=sh_c, passes=3, name="ml_proj_gc"), b_gates)
    k_scale = dk ** -0.5
    q_l, k_l = _conv3(qk_l, conv_w, conv_b, n_out=2, out_dtype=_BF16, act=True, rope=_rope_tables(l),
                      out_scales=(1.0, k_scale))
    (k_c,) = _conv3(k_c_raw, conv_w[:, qk_w:], conv_b[qk_w:], n_out=1, out_dtype=_BF16, act=True,
                    out_scales=(k_scale,))
    init = (jnp.zeros((heads, dk, dv), _F32), jnp.zeros((heads, 1, dk), _F32), jnp.full((heads, 1, 128), ML_M0, _F32))
    scan = functools.partial(_ml_scan, dk=dk, dv=dv)
    _, st_f = scan(None, k_c, v_c, g_c, init, reverse=False)
    _, st_b = scan(None, k_c, v_c, g_c, init, reverse=True)
    hl_f, _ = scan(q_l, k_l, v_l, g_l, st_f, reverse=False)
    hl_b, _ = scan(q_l, k_l, v_l, g_l, st_b, reverse=True)
    merged = _ml_merge(hl_f, hl_b, o_l, norm_w, dv)
    return _mm(merged, w_out.astype(_BF16), name="ml_out")


_SUBLANES = 8


def _sort_network(n):
    pairs = []
    p = 1
    while p < n:
        k = p
        while k >= 1:
            for j in range(k % p, n - k, 2 * k):
                for i in range(min(k, n - j - k)):
                    if (i + j) // (2 * p) == (i + j + k) // (2 * p):
                        pairs.append((i + j, i + j + k))
            k //= 2
        p *= 2
    return pairs


def _top_desc(x, count):
    n = x.shape[0] // _SUBLANES
    tiles = [x[j * _SUBLANES:(j + 1) * _SUBLANES, :] for j in range(n)]
    for a, b in _sort_network(n):
        tiles[a], tiles[b] = jnp.maximum(tiles[a], tiles[b]), jnp.minimum(tiles[a], tiles[b])
    slot = lax.broadcasted_iota(jnp.int32, tiles[0].shape, 0).astype(_F32)
    vals = []
    for it in range(count):
        m = jnp.max(tiles[0], axis=0, keepdims=True)
        vals.append(m)
        remaining = count - it - 1
        if remaining == 0:
            break
        first = jnp.min(jnp.where(tiles[0] == m, slot, float(_SUBLANES)), axis=0, keepdims=True)
        hit = slot == first
        for j in range(min(remaining, n)):
            below = tiles[j + 1] if j + 1 < n else -jnp.inf
            tiles[j] = jnp.where(hit, below, tiles[j])
    return vals


def _peer_route_kernel(q_ref, k1_ref, k2_ref, e1_ref, e2_ref, th_ref):
    half = k1_ref.shape[1]
    q = q_ref[...]
    s1 = _dot(k1_ref[...], q[:, :half], 3, dims=_NT)
    s2 = _dot(k2_ref[...], q[:, half:], 3, dims=_NT)
    n_top = PEER_TOPK + 1
    v1 = _top_desc(s1, n_top)
    v2 = _top_desc(s2, n_top)
    cands = [v1[a] + v2[b] for a in range(n_top) for b in range(n_top) if (a + 1) * (b + 1) <= n_top]
    pad = 64 - len(cands)
    cand = jnp.concatenate(cands + [jnp.full_like(cands[0], -jnp.inf)] * pad, axis=0)
    top = _top_desc(cand, n_top)
    m = top[0]
    z = top[0] * 0.0
    for j in range(PEER_TOPK):
        z = z + jnp.exp(top[j] - m)
    tau = 0.5 * (top[PEER_TOPK - 1] + top[PEER_TOPK])
    inv_z = 1.0 / z
    e1_ref[0] = jnp.exp(s1 - v1[0])
    e2_ref[0] = jnp.exp(s2 - v2[0]) * inv_z
    th_ref[0] = jnp.exp(tau - m) * inv_z


def _cast_t_kernel(x_ref, o_ref):
    o_ref[...] = x_ref[...].T.astype(o_ref.dtype)


def _cast_transpose(a, dtype):
    r, c = a.shape
    tr, tc = _tile(r, 1024, 128), _tile(c, 1024, 128)
    return pl.pallas_call(
        _cast_t_kernel,
        grid=(r // tr, c // tc),
        in_specs=[pl.BlockSpec((tr, tc), lambda i, j: (i, j))],
        out_specs=pl.BlockSpec((tc, tr), lambda i, j: (j, i)),
        out_shape=jax.ShapeDtypeStruct((c, r), dtype),
        compiler_params=_cp("parallel", "parallel"),
        name="cast_t",
    )(a)


def _rowwise_fp8(a):
    amax = jnp.max(jnp.abs(a), axis=1, keepdims=True)
    scale = jnp.maximum(amax, 1e-30) / _FP8_MAX
    return jnp.clip(a / scale, -_FP8_MAX, _FP8_MAX).astype(_FP8), scale


def _peer_dense_kernel(x_ref, sx_ref, u_ref, su_ref, vt_ref, e1_ref, e2_ref, th_ref, o_ref, act_ref, a_ref, *,
                       heads, nkeys):
    @pl.when(pl.program_id(1) == 0)
    def _():
        o_ref[...] = jnp.zeros_like(o_ref)

    n_sub = u_ref.shape[0] // nkeys
    n_part = 2
    part = u_ref.shape[0] // n_part
    lanes = 128

    def activations(pt):
        rows = slice(pt * part, (pt + 1) * part)
        act_ref[rows, :] = lax.dot_general(u_ref[rows, :], x_ref[...], _NT, preferred_element_type=_F32)

    def gated(i):
        rows = slice(i * nkeys, (i + 1) * nkeys)
        for c in range(x_ref.shape[0] // lanes):
            cols = slice(c * lanes, (c + 1) * lanes)
            gate = None
            for h in range(heads):
                p = e1_ref[h, i:i + 1, cols] * e2_ref[h, :, cols]
                sel = jnp.where(p >= th_ref[h, :, cols], p, 0.0)
                gate = sel if gate is None else gate + sel
            act = act_ref[rows, cols] * su_ref[rows, :] * sx_ref[:, cols]
            gelu = 0.5 * act * (1.0 + lax.erf(act * (2.0 ** -0.5)))
            a_ref[rows, cols] = (gate * gelu).astype(a_ref.dtype)

    activations(0)
    for pt in range(n_part):
        if pt + 1 < n_part:
            activations(pt + 1)
        for i in range(pt * n_sub // n_part, (pt + 1) * n_sub // n_part):
            gated(i)
        rows = slice(pt * part, (pt + 1) * part)
        o_ref[...] += jnp.dot(vt_ref[:, rows], a_ref[rows, :], preferred_element_type=_F32)


def _peer(x, wq, k1, k2, u_q, u_scale, vt_bf):
    t, d = x.shape
    heads, nkeys = PEER_HEADS, PEER_NKEYS
    qdim = wq.shape[1] // heads
    q = _mm(x, wq.astype(_BF16), name="peer_q")
    tt = _tile(t, 512, 128)
    e_shape = jax.ShapeDtypeStruct((heads, nkeys, t), _F32)
    e_blk = pl.BlockSpec((1, nkeys, tt), lambda i, h: (h, 0, i))
    th_blk = pl.BlockSpec((1, 1, tt), lambda i, h: (h, 0, i))
    e1, e2, th = pl.pallas_call(
        _peer_route_kernel,
        grid=(t // tt, heads),
        in_specs=[pl.BlockSpec((tt, qdim), lambda i, h: (i, h)),
                  pl.BlockSpec((nkeys, qdim // 2), lambda i, h: (0, 0)),
                  pl.BlockSpec((nkeys, qdim // 2), lambda i, h: (0, 0))],
        out_specs=[e_blk, e_blk, th_blk],
        out_shape=[e_shape, e_shape, jax.ShapeDtypeStruct((heads, 1, t), _F32)],
        compiler_params=_cp("parallel", "parallel"),
        name="peer_route",
    )(q, k1, k2)
    tb = _tile(t, 1024, 128)
    n_sub = 8
    eb = n_sub * nkeys
    x_q, x_scale = _rowwise_fp8(x)
    return pl.pallas_call(
        functools.partial(_peer_dense_kernel, heads=heads, nkeys=nkeys),
        grid=(t // tb, nkeys // n_sub),
        in_specs=[pl.BlockSpec((tb, d), lambda i, e: (i, 0)),
                  pl.BlockSpec((1, tb), lambda i, e: (0, i)),
                  pl.BlockSpec((eb, d), lambda i, e: (e, 0)),
                  pl.BlockSpec((eb, 128), lambda i, e: (e, 0)),
                  pl.BlockSpec((d, eb), lambda i, e: (0, e)),
                  pl.BlockSpec((heads, n_sub, tb), lambda i, e: (0, e, i)),
                  pl.BlockSpec((heads, nkeys, tb), lambda i, e: (0, 0, i)),
                  pl.BlockSpec((heads, 1, tb), lambda i, e: (0, 0, i))],
        out_specs=pl.BlockSpec((d, tb), lambda i, e: (0, i)),
        out_shape=jax.ShapeDtypeStruct((d, t), _F32),
        scratch_shapes=[pltpu.VMEM((eb, tb), _F32), pltpu.VMEM((eb, tb), _BF16)],
        compiler_params=_cp("parallel", "arbitrary"),
        name="peer_dense",
    )(x_q, x_scale.reshape(1, t), u_q, u_scale, vt_bf, e1, e2, th)


def _hyena_na_mixer(h_lat, h_ctx, mod_l, mod_c, w_in, conv_w, conv_b, fparams, skip, rpb, w_out):
    split = conv_w.shape[1]
    wb = w_in.astype(_BF16)
    w_hy, w_na = wb[:, :split], wb[:, split:]
    outs = []
    qkvs = []
    for hh, (sc, sh) in ((h_lat, mod_l), (h_ctx, mod_c)):
        p_hy = _mm(hh, w_hy, scale=sc, shift=sh, name="hy_proj")
        qkvs.append(_mm(hh, w_na, scale=sc, shift=sh, out_dtype=_BF16, name="na_proj"))
        outs.append(_hyena(p_hy, conv_w, conv_b, fparams, skip))
    na_lat = _natten(qkvs[0], qkvs[1], rpb)
    na_ctx = _ctx_attn(qkvs[1])
    wo = w_out.astype(_BF16)
    y_lat = _mm(jnp.concatenate([outs[0], na_lat], axis=-1), wo, name="ab_out")
    y_ctx = _mm(jnp.concatenate([outs[1], na_ctx], axis=-1), wo, name="ab_out")
    return y_lat, y_ctx


def kernel(x, c, ctx, c_ctx, ada_w, ada_b, ln_w, ln_b, ab_w_in, hy_conv_w, hy_conv_b, hy_fw1, hy_fb1, hy_fw2, hy_fb2, hy_fw3, hy_fb3, hy_fw4, hy_skip, na_rpb, ab_w_out, ml_w_in, ml_b_gates, ml_conv_w, ml_conv_b, ml_norm_w, ml_w_out, peer_wq, peer_k1, peer_k2, peer_u, peer_v):
    bsz, seq_len, d = x.shape
    depth = ada_w.shape[0]
    assert bsz == 1 and depth == 2, "one even (Hyena/attention) layer followed by one last odd (mLSTM) layer"
    alpha = (2 * depth) ** 0.25
    h_lat, h_ctx = x[0], ctx[0]
    cvec = jnp.zeros((8, d), _F32).at[0].set(c[0]).at[1].set(c_ctx)
    row = lambda a: a.reshape(1, d)
    for layer in range(depth):
        last = layer == depth - 1
        mod = _mm(cvec, ada_w[layer], pre="silu", bias=ada_b[layer].reshape(1, 6 * d), passes=3, name="ada_mod")
        sh_l, sc_l, g_l, shf_l, scf_l, gf_l = [mod[0:1, i * d:(i + 1) * d] for i in range(6)]
        sh_c, sc_c, g_c, shf_c, scf_c, gf_c = [mod[1:2, i * d:(i + 1) * d] for i in range(6)]
        if layer % 2 == 0:
            fparams = (hy_fw1[0], hy_fb1[0], hy_fw2[0], hy_fb2[0], hy_fw3[0], hy_fb3[0], hy_fw4[0])
            y_lat, y_ctx = _hyena_na_mixer(h_lat, h_ctx, (sc_l, sh_l), (sc_c, sh_c), ab_w_in[0], hy_conv_w[0],
                                           hy_conv_b[0], fparams, hy_skip[0], na_rpb[0], ab_w_out[0])
        else:
            y_lat = _mlstm_mixer(h_lat, h_ctx, (sc_l, sh_l), (sc_c, sh_c), ml_w_in[0], ml_b_gates[0], ml_conv_w[0],
                                 ml_conv_b[0], ml_norm_w[0], ml_w_out[0])
            y_ctx = None
        lw0, lb0, lw1, lb1 = row(ln_w[layer, 0]), row(ln_b[layer, 0]), row(ln_w[layer, 1]), row(ln_b[layer, 1])
        u_q, u_scale = _rowwise_fp8(peer_u[layer])
        u_scale = jnp.broadcast_to(u_scale, (u_scale.shape[0], 128))
        vt_bf = _cast_transpose(peer_v[layer], _BF16)
        peer = lambda v: _peer(v, peer_wq[layer], peer_k1[layer], peer_k2[layer], u_q, u_scale, vt_bf)
        h_lat, v_lat = _ln_res(h_lat, y_lat, g_l, lw0, lb0, scf_l, shf_l, alpha=alpha, with_mod=True)
        h_lat, _ = _ln_res(h_lat, peer(v_lat), gf_l, lw1, lb1, scf_l, shf_l, alpha=alpha, with_mod=False,
                           y_transposed=True)
        if not last:
            h_ctx, v_ctx = _ln_res(h_ctx, y_ctx, g_c, lw0, lb0, scf_c, shf_c, alpha=alpha, with_mod=True)
            h_ctx, _ = _ln_res(h_ctx, peer(v_ctx), gf_c, lw1, lb1, scf_c, shf_c, alpha=alpha, with_mod=False,
                               y_transposed=True)
    return h_lat[None]
```

```python
import functools
import math

import numpy as np
import jax
import jax.numpy as jnp
from jax import lax
from jax.experimental import pallas as pl
from jax.experimental.pallas import tpu as pltpu

_F32 = jnp.float32
_BF16 = jnp.bfloat16
_FP8 = jnp.float8_e4m3fn
_FP8_MAX = 448.0

GRID_W = 64
HY_ORDER = 2
HY_EMB = 33
HY_DECAY_TARGET = 1e-2
HY_FAST_PCT = 0.3
HY_SLOW_PCT = 1.5
NA_HEADS = 8
NA_WIN_ROWS = 8
NA_WIN_COLS = 16
ML_HEADS = 8
ML_TILE = 256
ML_M0 = -1e30
ROPE_BASE = 10000.0
PEER_HEADS = 8
PEER_NKEYS = 128
PEER_TOPK = 16
LN_EPS = 1e-6

_VMEM_LIMIT = 52 * 1024 * 1024
_NEG = -1e30
_FFT_DT = jnp.bfloat16

_NT = (((1,), (1,)), ((), ()))
_TN = (((0,), (0,)), ((), ()))


def _cp(*sem):
    return pltpu.CompilerParams(dimension_semantics=sem, vmem_limit_bytes=_VMEM_LIMIT)


def _tile(n, cap, align):
    t = (min(cap, n) // align) * align
    while t >= align:
        if n % t == 0:
            return t
        t -= align
    return n


def _split(x):
    hi = x.astype(_BF16)
    lo = (x.astype(_F32) - hi.astype(_F32)).astype(_BF16)
    return hi, lo


def _dot(a, b, passes=1, dims=None):
    dn = dims or (((a.ndim - 1,), (0,)), ((), ()))
    f = lambda x, y: lax.dot_general(x, y, dn, preferred_element_type=_F32)
    if passes == 1:
        return f(a.astype(_BF16), b.astype(_BF16))
    ah, al = _split(a)
    bh, bl = _split(b)
    return f(ah, bh) + (f(ah, bl) + f(al, bh))


def _sigmoid(x):
    return 1.0 / (1.0 + jnp.exp(-x))


def _mm_kernel(*refs, pre, has_bias, passes):
    it = iter(refs)
    a_ref = next(it)
    if pre == "mod":
        sc_ref, sh_ref = next(it), next(it)
    b_ref = next(it)
    bias_ref = next(it) if has_bias else None
    o_ref = next(it)
    a = a_ref[...]
    if pre == "mod":
        a = a.astype(_F32) * (1.0 + sc_ref[...]) + sh_ref[...]
    elif pre == "silu":
        a = a * _sigmoid(a)
    acc = _dot(a, b_ref[...], passes)
    if has_bias:
        acc = acc + bias_ref[...]
    o_ref[...] = acc.astype(o_ref.dtype)


def _mm(a, b, *, scale=None, shift=None, pre=None, bias=None, out_dtype=_F32, passes=1, name="mm"):
    m, k = a.shape
    n = b.shape[1]
    if scale is not None:
        pre = "mod"
    tile_bytes = 4 * 1024 * 1024
    tm = _tile(m, max(8, min(1024, 2 * tile_bytes // (k * a.dtype.itemsize))), 8)
    tn = _tile(n, max(128, min(8192, tile_bytes // (k * b.dtype.itemsize), tile_bytes // (tm * 4))), 128)
    in_specs = [pl.BlockSpec((tm, k), lambda i, j: (i, 0))]
    args = [a]
    if pre == "mod":
        in_specs += [pl.BlockSpec((1, k), lambda i, j: (0, 0))] * 2
        args += [scale, shift]
    in_specs.append(pl.BlockSpec((k, tn), lambda i, j: (0, j)))
    args.append(b)
    if bias is not None:
        in_specs.append(pl.BlockSpec((1, tn), lambda i, j: (0, j)))
        args.append(bias)
    return pl.pallas_call(
        functools.partial(_mm_kernel, pre=pre, has_bias=bias is not None, passes=passes),
        grid=(m // tm, n // tn),
        in_specs=in_specs,
        out_specs=pl.BlockSpec((tm, tn), lambda i, j: (i, j)),
        out_shape=jax.ShapeDtypeStruct((m, n), out_dtype),
        compiler_params=_cp("parallel", "parallel"),
        name=name,
    )(*args)


def _ln_res_kernel(h_ref, y_ref, g_ref, w_ref, b_ref, sc_ref, sh_ref, *o_refs, alpha, with_mod, y_transposed):
    y = y_ref[...].T if y_transposed else y_ref[...]
    z = alpha * h_ref[...] + g_ref[...] * y
    mu = jnp.mean(z, axis=-1, keepdims=True)
    zc = z - mu
    var = jnp.mean(zc * zc, axis=-1, keepdims=True)
    hn = zc * lax.rsqrt(var + LN_EPS) * w_ref[...] + b_ref[...]
    o_refs[0][...] = hn
    if with_mod:
        o_refs[1][...] = hn * (1.0 + sc_ref[...]) + sh_ref[...]


def _ln_res(h, y, g, w, b, sc, sh, *, alpha, with_mod, y_transposed=False):
    m, d = h.shape
    tr = _tile(m, 256, 128 if y_transposed else 8)
    row = pl.BlockSpec((tr, d), lambda i: (i, 0))
    y_spec = pl.BlockSpec((d, tr), lambda i: (0, i)) if y_transposed else row
    vec = pl.BlockSpec((1, d), lambda i: (0, 0))
    n_out = 2 if with_mod else 1
    out = pl.pallas_call(
        functools.partial(_ln_res_kernel, alpha=alpha, with_mod=with_mod, y_transposed=y_transposed),
        grid=(m // tr,),
        in_specs=[row, y_spec, vec, vec, vec, vec, vec],
        out_specs=[row] * n_out,
        out_shape=[jax.ShapeDtypeStruct((m, d), _F32)] * n_out,
        compiler_params=_cp("parallel"),
        name="ln_res",
    )(h, y, g, w, b, sc, sh)
    return out if with_mod else (out[0], None)


def _conv3_kernel(*refs, n_out, act, rope, out_scales):
    it = iter(refs)
    xp_ref, x_ref, xn_ref, w_ref, b_ref = (next(it) for _ in range(5))
    if rope:
        cos_ref, sin_ref = next(it), next(it)
    o_refs = [next(it) for _ in range(n_out)]
    i = pl.program_id(0)
    x = x_ref[...]
    tr = x.shape[0]
    rows = lax.broadcasted_iota(jnp.int32, x.shape, 0)
    prev_row = jnp.where(i > 0, xp_ref[7:8, :], 0.0)
    next_row = jnp.where(i < pl.num_programs(0) - 1, xn_ref[0:1, :], 0.0)
    x_m1 = jnp.where(rows == 0, prev_row, pltpu.roll(x, 1, axis=0))
    x_p1 = jnp.where(rows == tr - 1, next_row, pltpu.roll(x, tr - 1, axis=0))
    y = b_ref[...] + x_m1 * w_ref[0:1, :] + x * w_ref[1:2, :] + x_p1 * w_ref[2:3, :]
    if act:
        y = y * _sigmoid(y)
    wo = y.shape[1] // n_out
    for o in range(n_out):
        yo = y[:, o * wo:(o + 1) * wo]
        if rope:
            cos, sin = cos_ref[...], sin_ref[...]
            lane = lax.broadcasted_iota(jnp.int32, cos.shape, 1)
            first = (lane % 64) < 32
            parts = []
            for hh in range(wo // 128):
                yh = yo[:, hh * 128:(hh + 1) * 128]
                sw = jnp.where(first, pltpu.roll(yh, 96, axis=1), pltpu.roll(yh, 32, axis=1))
                parts.append(yh * cos + sw * sin)
            yo = jnp.concatenate(parts, axis=1)
        if out_scales[o] != 1.0:
            yo = yo * out_scales[o]
        o_refs[o][...] = yo.astype(o_refs[o].dtype)


def _conv3(x, w, b, *, n_out, out_dtype, act=False, rope=None, out_scales=None):
    l, cx = x.shape
    tr = _tile(l, 256, 8)
    nb8 = l // 8
    out_scales = out_scales or (1.0,) * n_out
    in_specs = [
        pl.BlockSpec((8, cx), lambda i: (jnp.maximum(i * (tr // 8) - 1, 0), 0)),
        pl.BlockSpec((tr, cx), lambda i: (i, 0)),
        pl.BlockSpec((8, cx), lambda i: (jnp.minimum((i + 1) * (tr // 8), nb8 - 1), 0)),
        pl.BlockSpec((3, cx), lambda i: (0, 0)),
        pl.BlockSpec((1, cx), lambda i: (0, 0)),
    ]
    args = [x, x, x, w, b.reshape(1, cx)]
    if rope is not None:
        in_specs += [pl.BlockSpec((tr, 128), lambda i: (i, 0))] * 2
        args += list(rope)
    wo = cx // n_out
    return pl.pallas_call(
        functools.partial(_conv3_kernel, n_out=n_out, act=act, rope=rope is not None, out_scales=out_scales),
        grid=(l // tr,),
        in_specs=in_specs,
        out_specs=[pl.BlockSpec((tr, wo), lambda i: (i, 0))] * n_out,
        out_shape=[jax.ShapeDtypeStruct((l, wo), out_dtype)] * n_out,
        compiler_params=_cp("parallel"),
        name="conv3",
    )(*args)


def _hy_filter_kernel(f_ref, t_ref, w1_ref, b1_ref, w2_ref, b2_ref, w3_ref, b3_ref, w4_ref, d_ref, o_ref):
    h = jnp.sin(_dot(f_ref[...], w1_ref[...], 3) + b1_ref[...])
    h = jnp.sin(_dot(h, w2_ref[...], 3) + b2_ref[...])
    h = jnp.sin(_dot(h, w3_ref[...], 3) + b3_ref[...])
    taps = _dot(h, w4_ref[...], 3)
    tv = t_ref[...]
    window = jnp.exp(-tv[:, 0:1] * d_ref[...]) * tv[:, 1:2]
    c = window.shape[1]
    for g in range(taps.shape[1] // c):
        o_ref[:, g * c:(g + 1) * c] = (taps[:, g * c:(g + 1) * c] * window).astype(o_ref.dtype)


def _hy_taps(seq_len, l_pad, fw1, fb1, fw2, fb2, fw3, fb3, fw4, width):
    hid = fw1.shape[1]
    n = np.arange(2 * l_pad)
    off = np.where(n < l_pad, n, 2 * l_pad - n)
    exists = np.where(n < l_pad, off < seq_len, (off >= 1) & (off < seq_len))
    pos = jnp.asarray(np.where(exists, off, 0), _F32)
    t = pos / (seq_len - 1)
    omega = 2.0 * math.pi * pos / seq_len
    t = jnp.stack([t, jnp.asarray(exists, _F32)], axis=-1)
    bands = (HY_EMB - 1) // 2
    freqs = jnp.linspace(1e-4, bands - 1, bands, dtype=_F32)
    ang = omega[:, None] * freqs[None, :]
    kpad = 64
    feats = jnp.concatenate([t[:, 0:1], jnp.cos(ang), -jnp.sin(ang), jnp.zeros((2 * l_pad, kpad - HY_EMB), _F32)],
                            axis=-1)
    w1 = jnp.concatenate([fw1, jnp.zeros((kpad - HY_EMB, hid), _F32)], axis=0)
    max_decay = math.log(HY_DECAY_TARGET) / HY_FAST_PCT
    min_decay = math.log(HY_DECAY_TARGET) / HY_SLOW_PCT
    deltas = jnp.abs(jnp.linspace(min_decay, max_decay, width, dtype=_F32))[None, :]
    n_cols = fw4.shape[1] // 2
    tl = _tile(l_pad, 256, 8)
    per_dir = l_pad // tl
    full = lambda shape: pl.BlockSpec(shape, lambda i: (0,) * len(shape))
    return pl.pallas_call(
        _hy_filter_kernel,
        grid=(2 * per_dir,),
        in_specs=[pl.BlockSpec((tl, kpad), lambda i: (i, 0)), pl.BlockSpec((tl, 2), lambda i: (i, 0)),
                  full((kpad, hid)), full((1, hid)), full((hid, hid)), full((1, hid)), full((hid, hid)), full((1, hid)),
                  pl.BlockSpec((hid, n_cols), lambda i: (0, i // per_dir)), full((1, width))],
        out_specs=pl.BlockSpec((tl, n_cols), lambda i: (i, 0)),
        out_shape=jax.ShapeDtypeStruct((2 * l_pad, n_cols), _BF16),
        compiler_params=_cp("parallel"),
        name="hy_filter",
    )(feats, t, w1, fb1.reshape(1, hid), fw2, fb2.reshape(1, hid), fw3, fb3.reshape(1, hid), fw4, deltas)


def _fft_consts(n1, n2):
    n = n1 * n2
    k1 = np.arange(n1)[:, None]
    j1 = np.arange(n1 // 2)[None, :]
    ang1 = 2.0 * np.pi * k1 * j1 / n1
    f1_fwd = np.concatenate([np.cos(ang1), -np.sin(ang1)], axis=0)
    f1_inv = np.concatenate([np.cos(ang1).T, -np.sin(ang1).T], axis=1) / n
    k2 = np.arange(n2)[:, None]
    j2 = np.arange(n2)[None, :]
    ang2 = 2.0 * np.pi * k2 * j2 / n2
    f2 = np.stack([np.cos(ang2), -np.sin(ang2)])
    angt = 2.0 * np.pi * np.arange(n1)[:, None] * np.arange(n2)[None, :] / n
    tw = np.stack([np.cos(angt), -np.sin(angt)], axis=1)
    angf = 2.0 * np.pi * k1 * np.arange(n1)[None, :] / n1
    f1_full = np.concatenate([np.cos(angf), -np.sin(angf)], axis=0)
    as32 = lambda a: jnp.asarray(a, dtype=_F32)
    return as32(f1_fwd), as32(f1_inv), as32(f2), as32(tw), as32(np.transpose(tw, (0, 2, 1))), as32(f1_full)


def _cmm(f_stack, xr, xi):
    n = f_stack.shape[0] // 2
    p1 = jnp.dot(f_stack, xr.astype(_BF16), preferred_element_type=_F32)
    p2 = jnp.dot(f_stack, xi.astype(_BF16), preferred_element_type=_F32)
    return p1[:n] - p2[n:], p2[:n] + p1[n:]


def _twiddled_dft(f2_ref, tw_ref):
    f2r, f2i = f2_ref[0], f2_ref[1]
    tr, ti = tw_ref[0, 0:1, :], tw_ref[0, 1:2, :]
    return jnp.concatenate([f2r * tr - f2i * ti, f2r * ti + f2i * tr], axis=0).astype(_BF16)


def _fft_spec_kernel(a_ref, f2_ref, tw_ref, o_ref):
    fw = _twiddled_dft(f2_ref, tw_ref)
    hr, hi = _cmm(fw, a_ref[0, 0], a_ref[1, 0])
    o_ref[0, 0, 0] = hr.astype(o_ref.dtype)
    o_ref[0, 1, 0] = hi.astype(o_ref.dtype)


def _fft_mid_kernel(a_ref, h_ref, f2_ref, tw_ref, twc_ref, o_ref):
    fw = _twiddled_dft(f2_ref, tw_ref)
    br, bi = _cmm(fw, a_ref[0, 0], a_ref[1, 0])
    hr, hi = h_ref[0, 0, 0].astype(_F32), h_ref[0, 1, 0].astype(_F32)
    yr = (br * hr - bi * hi).astype(_BF16)
    yi = (br * hi + bi * hr).astype(_BF16)
    f2r, f2i = f2_ref[0], f2_ref[1]
    tcr, tci = twc_ref[0, :, 0:1], twc_ref[0, :, 1:2]
    g = jnp.concatenate([tcr * f2r - tci * f2i, -(tcr * f2i + tci * f2r)], axis=0).astype(_BF16)
    zr, zi = _cmm(g, yr, yi)
    o_ref[0, 0] = zr.astype(o_ref.dtype)
    o_ref[1, 0] = zi.astype(o_ref.dtype)


def _fft_shape(l_pad):
    n2 = 256 if l_pad >= 8192 else 64
    n1 = 2 * l_pad // n2
    return n1, n2


def _hy_spectra(taps, width, consts, n1, n2):
    f1_fwd, _, f2, tw, _, f1_full = consts
    half = taps.shape[1]
    a = _mm(f1_full, taps.reshape(n1, n2 * half), out_dtype=_FFT_DT, name="fft_a_filt")
    a = a.reshape(2, n1, n2, half)
    return pl.pallas_call(
        _fft_spec_kernel,
        grid=(n1, HY_ORDER),
        in_specs=[pl.BlockSpec((2, 1, n2, width), lambda k, o: (0, k, 0, o)),
                  pl.BlockSpec((2, n2, n2), lambda k, o: (0, 0, 0)),
                  pl.BlockSpec((1, 2, n2), lambda k, o: (k, 0, 0))],
        out_specs=pl.BlockSpec((1, 2, 1, n2, width), lambda k, o: (o, 0, k, 0, 0)),
        out_shape=jax.ShapeDtypeStruct((HY_ORDER, 2, n1, n2, width), _FFT_DT),
        compiler_params=_cp("parallel", "parallel"),
        name="fft_spec",
    )(a, f2, tw)


def _long_conv(z, spec, order, consts, n1, n2):
    f1_fwd, f1_inv, f2, tw, twc, _ = consts
    l_pad, c = z.shape
    a = _mm(f1_fwd, z.reshape(n1 // 2, n2 * c), out_dtype=_FFT_DT, name="fft_a")
    a = a.reshape(2, n1, n2, c)
    blk = pl.BlockSpec((2, 1, n2, c), lambda k: (0, k, 0, 0))
    mid = pl.pallas_call(
        _fft_mid_kernel,
        grid=(n1,),
        in_specs=[blk, pl.BlockSpec((1, 2, 1, n2, c), lambda k: (order, 0, k, 0, 0)),
                  pl.BlockSpec((2, n2, n2), lambda k: (0, 0, 0)),
                  pl.BlockSpec((1, 2, n2), lambda k: (k, 0, 0)), pl.BlockSpec((1, n2, 2), lambda k: (k, 0, 0))],
        out_specs=blk,
        out_shape=jax.ShapeDtypeStruct((2, n1, n2, c), _FFT_DT),
        compiler_params=_cp("parallel"),
        name="fft_mid",
    )(a, spec, f2, tw, twc)
    y = _mm(f1_inv, mid.reshape(2 * n1, n2 * c), name="fft_c")
    return y.reshape(l_pad, c)


def _hy_gate_kernel(y_ref, z_ref, s_ref, g_ref, o_ref):
    o_ref[...] = (g_ref[...] * (y_ref[...] + z_ref[...] * s_ref[...])).astype(o_ref.dtype)


def _hy_gate(y, z, skip, gate, out_dtype):
    l, c = y.shape
    tr = _tile(l, 512, 8)
    row = pl.BlockSpec((tr, c), lambda i: (i, 0))
    return pl.pallas_call(
        _hy_gate_kernel,
        grid=(l // tr,),
        in_specs=[row, row, pl.BlockSpec((1, c), lambda i: (0, 0)), row],
        out_specs=row,
        out_shape=jax.ShapeDtypeStruct((l, c), out_dtype),
        compiler_params=_cp("parallel"),
        name="hy_gate",
    )(y, z, skip.reshape(1, c), gate)


def _hyena(p, conv_w, conv_b, fparams, skip):
    l = p.shape[0]
    width = p.shape[1] // (HY_ORDER + 1)
    l_pad = max(l, 1024)
    n1, n2 = _fft_shape(l_pad)
    consts = _fft_consts(n1, n2)
    taps = _hy_taps(l, l_pad, *fparams, width)
    parts = _conv3(p, conv_w, conv_b, n_out=HY_ORDER + 1, out_dtype=_F32)
    if l_pad != l:
        parts = [jnp.pad(a, ((0, l_pad - l), (0, 0))) for a in parts]
    spec = _hy_spectra(taps, width, consts, n1, n2)
    z = parts[0]
    for o in range(HY_ORDER):
        y = _long_conv(z, spec, o, consts, n1, n2)
        z = _hy_gate(y, z, skip[o], parts[o + 1], _F32 if o + 1 < HY_ORDER else _BF16)
    return z[:l]


def _na_bias_table(rpb, w):
    kr, kc = NA_WIN_ROWS, NA_WIN_COLS
    cols = np.arange(w)
    start = np.clip(cols - kc // 2, 0, w - kc)
    kcol = np.arange(w)[None, :]
    in_win = (kcol >= start[:, None]) & (kcol < start[:, None] + kc)
    rp = jnp.pad(rpb, ((0, 0), (0, 0), (w - 1, w)))
    by_q = jnp.stack([rp[:, :, kc - 1 + w - 1 - q:kc - 1 + 2 * w - 1 - q] for q in range(w)], axis=2)
    by_q = jnp.where(in_win[None, None], by_q, _NEG)
    t = jnp.stack([jnp.concatenate([by_q[:, i - d + kr - 1] for i in range(kr)], axis=-1) for d in range(kr)], axis=1)
    return t


def _natten_kernel(q_ref, k_ref, v_ref, kc_ref, vc_ref, bias_ref, o_ref, *, rows, w, scale):
    qb = pl.program_id(1)
    kr = NA_WIN_ROWS
    kc, vc = kc_ref[...], vc_ref[...]
    for a in range(kr):
        r = qb * kr + a
        r0 = jnp.clip(r - kr // 2, 0, rows - kr)
        d = r - r0
        start = pl.multiple_of(r0 * w, w)
        kwin = k_ref[pl.ds(start, kr * w), :]
        vwin = v_ref[pl.ds(start, kr * w), :]
        qa = q_ref[a * w:(a + 1) * w, :]
        s_lat = _dot(qa, kwin, dims=_NT) * scale + bias_ref[0, d]
        s_ctx = _dot(qa, kc, dims=_NT) * scale
        m = jnp.maximum(jnp.max(s_lat, axis=-1, keepdims=True), jnp.max(s_ctx, axis=-1, keepdims=True))
        p_lat = jnp.exp(s_lat - m)
        p_ctx = jnp.exp(s_ctx - m)
        den = jnp.sum(p_lat, axis=-1, keepdims=True) + jnp.sum(p_ctx, axis=-1, keepdims=True)
        o = _dot(p_lat, vwin) + _dot(p_ctx, vc)
        o_ref[a * w:(a + 1) * w, :] = (o / den).astype(o_ref.dtype)


def _natten(qkv, kv_ctx, rpb):
    l = qkv.shape[0]
    h = NA_HEADS
    hd = qkv.shape[1] // (3 * h)
    w = GRID_W
    rows = l // w
    kr = NA_WIN_ROWS
    assert rows % kr == 0 and rows >= kr and hd % 128 == 0
    lc = kv_ctx.shape[0]
    bias = _na_bias_table(rpb, w)
    tq = kr * w
    return pl.pallas_call(
        functools.partial(_natten_kernel, rows=rows, w=w, scale=hd ** -0.5),
        grid=(h, rows // kr),
        in_specs=[pl.BlockSpec((tq, hd), lambda hh, qb: (qb, hh)),
                  pl.BlockSpec((l, hd), lambda hh, qb: (0, h + hh)),
                  pl.BlockSpec((l, hd), lambda hh, qb: (0, 2 * h + hh)),
                  pl.BlockSpec((lc, hd), lambda hh, qb: (0, h + hh)),
                  pl.BlockSpec((lc, hd), lambda hh, qb: (0, 2 * h + hh)),
                  pl.BlockSpec((1, kr, w, kr * w), lambda hh, qb: (hh, 0, 0, 0))],
        out_specs=pl.BlockSpec((tq, hd), lambda hh, qb: (qb, hh)),
        out_shape=jax.ShapeDtypeStruct((l, h * hd), _BF16),
        compiler_params=_cp("parallel", "parallel"),
        name="natten",
    )(qkv, qkv, qkv, kv_ctx, kv_ctx, bias)


def _ctx_attn_kernel(q_ref, k_ref, v_ref, o_ref, *, scale):
    s = _dot(q_ref[...], k_ref[...], dims=_NT) * scale
    p = jnp.exp(s - jnp.max(s, axis=-1, keepdims=True))
    den = jnp.sum(p, axis=-1, keepdims=True)
    o_ref[...] = (_dot(p, v_ref[...]) / den).astype(o_ref.dtype)


def _ctx_attn(qkv):
    lc = qkv.shape[0]
    h = NA_HEADS
    hd = qkv.shape[1] // (3 * h)
    blk = lambda off: pl.BlockSpec((lc, hd), lambda hh: (0, off + hh))
    return pl.pallas_call(
        functools.partial(_ctx_attn_kernel, scale=hd ** -0.5),
        grid=(h,),
        in_specs=[blk(0), blk(h), blk(2 * h)],
        out_specs=blk(0),
        out_shape=jax.ShapeDtypeStruct((lc, h * hd), _BF16),
        compiler_params=_cp("parallel"),
        name="ctx_attn",
    )(qkv, qkv, qkv)


def _ml_gates_kernel(g_ref, b_ref, o_ref, *, heads):
    g = g_ref[...] + b_ref[...]
    logsig = jnp.minimum(g, 0.0) - jnp.log(1.0 + jnp.exp(-jnp.abs(g)))
    lane = lax.broadcasted_iota(jnp.int32, g.shape, 1)
    is_forget = (lane // heads) % 2 == 1
    o_ref[...] = jnp.where(is_forget, logsig, g)


def _ml_gates(g_raw, b_gates):
    l, n = g_raw.shape
    tr = _tile(l, 1024, 8)
    bias = jnp.pad(b_gates, (0, n - b_gates.shape[0])).reshape(1, n)
    return pl.pallas_call(
        functools.partial(_ml_gates_kernel, heads=ML_HEADS),
        grid=(l // tr,),
        in_specs=[pl.BlockSpec((tr, n), lambda i: (i, 0)), pl.BlockSpec((1, n), lambda i: (0, 0))],
        out_specs=pl.BlockSpec((tr, n), lambda i: (i, 0)),
        out_shape=jax.ShapeDtypeStruct((l, n), _F32),
        compiler_params=_cp("parallel"),
        name="ml_gates",
    )(g_raw, bias)


def _ml_scan_kernel(*refs, heads, dk, dv, reverse, with_h):
    it = iter(refs)
    if with_h:
        q_ref = next(it)
    k_ref, v_ref, g_ref, gt_ref, c0_ref, n0_ref, m0_ref = (next(it) for _ in range(7))
    if with_h:
        h_ref = next(it)
    c_ref, n_ref, m_ref = next(it), next(it), next(it)

    @pl.when(pl.program_id(0) == 0)
    def _():
        c_ref[...] = c0_ref[...]
        n_ref[...] = n0_ref[...]
        m_ref[...] = m0_ref[...]

    t = k_ref.shape[0]
    ti = lax.broadcasted_iota(jnp.int32, (t, t), 0)
    si = lax.broadcasted_iota(jnp.int32, (t, t), 1)
    seen = (si >= ti) if reverse else (si <= ti)
    seen_t = (ti >= si) if reverse else (ti <= si)
    g = g_ref[...]
    gt = gt_ref[0]
    c0 = 2 * heads if reverse else 0
    for h in range(heads):
        i_col = g[:, c0 + h:c0 + h + 1]
        f_col = g[:, c0 + heads + h:c0 + heads + h + 1]
        i_row = gt[c0 + h:c0 + h + 1, :]
        f_row = gt[c0 + heads + h:c0 + heads + h + 1, :]
        b_col = jnp.sum(jnp.where(seen, f_row, 0.0), axis=1, keepdims=True)
        b_row = jnp.sum(jnp.where(seen_t, f_col, 0.0), axis=0, keepdims=True)
        b_end = jnp.sum(f_row, axis=1, keepdims=True)
        m_prev = m_ref[h][:, 0:1]
        w_log_row = b_end - b_row + i_row
        w_log_col = b_end - b_col + i_col
        m_new = jnp.maximum(b_end + m_prev, jnp.max(w_log_row, axis=1, keepdims=True))
        decay = jnp.exp(b_end + m_prev - m_new)
        w_col = jnp.exp(w_log_col - m_new)
        kh = k_ref[:, h * dk:(h + 1) * dk]
        vh = v_ref[:, h * dv:(h + 1) * dv]
        ct = c_ref[h]
        nv = n_ref[h]
        if with_h:
            qh = q_ref[:, h * dk:(h + 1) * dk]
            log_d = jnp.where(seen, b_col - b_row + i_row, -jnp.inf)
            m_in = b_col + m_prev
            m_t = jnp.maximum(m_in, jnp.max(log_d, axis=1, keepdims=True))
            sc = _dot(qh, kh, dims=_NT) * jnp.exp(log_d - m_t)
            g_in = jnp.exp(m_in - m_t)
            num = _dot(sc, vh) + _dot(qh, ct) * g_in
            den = jnp.sum(sc, axis=1, keepdims=True) + jnp.sum(qh.astype(_F32) * nv, axis=1, keepdims=True) * g_in
            h_ref[:, h * dv:(h + 1) * dv] = num / jnp.maximum(jnp.abs(den), jnp.exp(-m_t))
        vw = vh.astype(_F32) * w_col
        c_ref[h] = decay * ct + _dot(kh, vw, dims=_TN)
        n_ref[h] = decay * nv + jnp.sum(kh.astype(_F32) * w_col, axis=0, keepdims=True)
        m_ref[h] = jnp.broadcast_to(m_new, m_ref.shape[1:])


def _ml_scan(q, k, v, gates, state, *, reverse, dk, dv):
    l = k.shape[0]
    heads = ML_HEADS
    t = min(ML_TILE, l)
    assert l % t == 0
    nc = l // t
    with_h = q is not None
    gt = jnp.transpose(gates.reshape(nc, t, gates.shape[1]), (0, 2, 1))
    cidx = (lambda j: nc - 1 - j) if reverse else (lambda j: j)
    row = lambda width: pl.BlockSpec((t, width), lambda j: (cidx(j), 0))
    st_specs = [pl.BlockSpec((heads, dk, dv), lambda j: (0, 0, 0)), pl.BlockSpec((heads, 1, dk), lambda j: (0, 0, 0)),
                pl.BlockSpec((heads, 1, 128), lambda j: (0, 0, 0))]
    st_shapes = [jax.ShapeDtypeStruct((heads, dk, dv), _F32), jax.ShapeDtypeStruct((heads, 1, dk), _F32),
                 jax.ShapeDtypeStruct((heads, 1, 128), _F32)]
    in_specs = ([row(heads * dk)] if with_h else []) + [
        row(heads * dk), row(heads * dv), row(gates.shape[1]),
        pl.BlockSpec((1, gates.shape[1], t), lambda j: (cidx(j), 0, 0))] + st_specs
    args = ([q] if with_h else []) + [k, v, gates, gt, *state]
    out_specs = ([row(heads * dv)] if with_h else []) + st_specs
    out_shape = ([jax.ShapeDtypeStruct((l, heads * dv), _F32)] if with_h else []) + st_shapes
    out = pl.pallas_call(
        functools.partial(_ml_scan_kernel, heads=heads, dk=dk, dv=dv, reverse=reverse, with_h=with_h),
        grid=(nc,),
        in_specs=in_specs,
        out_specs=out_specs,
        out_shape=out_shape,
        compiler_params=_cp("arbitrary"),
        name="ml_scan",
    )(*args)
    return (out[0], tuple(out[1:])) if with_h else (None, tuple(out))


def _ml_merge_kernel(hf_ref, hb_ref, o_ref, w_ref, out_ref, *, heads, dv):
    for h in range(heads):
        sl = slice(h * dv, (h + 1) * dv)
        x = hf_ref[:, sl] + hb_ref[:, sl]
        mu = jnp.mean(x, axis=-1, keepdims=True)
        xc = x - mu
        var = jnp.mean(xc * xc, axis=-1, keepdims=True)
        hn = xc * lax.rsqrt(var + LN_EPS) * w_ref[:, sl]
        out_ref[:, sl] = (hn * _sigmoid(o_ref[:, sl])).astype(out_ref.dtype)


def _ml_merge(hf, hb, o, norm_w, dv):
    l, n = hf.shape
    tr = _tile(l, 256, 8)
    row = pl.BlockSpec((tr, n), lambda i: (i, 0))
    return pl.pallas_call(
        functools.partial(_ml_merge_kernel, heads=ML_HEADS, dv=dv),
        grid=(l // tr,),
        in_specs=[row, row, row, pl.BlockSpec((1, n), lambda i: (0, 0))],
        out_specs=row,
        out_shape=jax.ShapeDtypeStruct((l, n), _BF16),
        compiler_params=_cp("parallel"),
        name="ml_merge",
    )(hf, hb, o, norm_w.reshape(1, n))


def _rope_tables(l):
    quarter = 32
    pos = jnp.arange(l)
    inv = ROPE_BASE ** (-jnp.arange(quarter, dtype=_F32) / quarter)
    ang_r = (pos // GRID_W).astype(_F32)[:, None] * inv[None]
    ang_c = (pos % GRID_W).astype(_F32)[:, None] * inv[None]
    cos = jnp.concatenate([jnp.cos(ang_r)] * 2 + [jnp.cos(ang_c)] * 2, axis=-1)
    sin = jnp.concatenate([-jnp.sin(ang_r), jnp.sin(ang_r), -jnp.sin(ang_c), jnp.sin(ang_c)], axis=-1)
    return cos, sin


def _mlstm_mixer(h_lat, h_ctx, mod_l, mod_c, w_in, b_gates, conv_w, conv_b, norm_w, w_out):
    heads = ML_HEADS
    v_w = w_out.shape[0]
    dv = v_w // heads
    dk = dv // 2
    qk_w = heads * dk
    g0 = 2 * qk_w + 2 * v_w
    n_g = w_in.shape[1] - g0
    assert dk == 128
    l = h_lat.shape[0]
    wb = w_in.astype(_BF16)
    w_g = jnp.pad(w_in[:, g0:], ((0, 0), (0, 128 - n_g)))
    sc_l, sh_l = mod_l
    sc_c, sh_c = mod_c
    qk_l = _mm(h_lat, wb[:, :2 * qk_w], scale=sc_l, shift=sh_l, name="ml_proj_qk")
    v_l = _mm(h_lat, wb[:, 2 * qk_w:2 * qk_w + v_w], scale=sc_l, shift=sh_l, out_dtype=_BF16, name="ml_proj_v")
    o_l = _mm(h_lat, wb[:, 2 * qk_w + v_w:g0], scale=sc_l, shift=sh_l, name="ml_proj_o")
    g_l = _ml_gates(_mm(h_lat, w_g, scale=sc_l, shift=sh_l, passes=3, name="ml_proj_g"), b_gates)
    k_c_raw = _mm(h_ctx, wb[:, qk_w:2 * qk_w], scale=sc_c, shift=sh_c, name="ml_proj_kc")
    v_c = _mm(h_ctx, wb[:, 2 * qk_w:2 * qk_w + v_w], scale=sc_c, shift=sh_c, out_dtype=_BF16, name="ml_proj_vc")
    g_c = _ml_gates(_mm(h_ctx, w_g, scale=sc_c, shift=sh_c, passes=3, name="ml_proj_gc"), b_gates)
    k_scale = dk ** -0.5
    q_l, k_l = _conv3(qk_l, conv_w, conv_b, n_out=2, out_dtype=_BF16, act=True, rope=_rope_tables(l),
                      out_scales=(1.0, k_scale))
    (k_c,) = _conv3(k_c_raw, conv_w[:, qk_w:], conv_b[qk_w:], n_out=1, out_dtype=_BF16, act=True,
                    out_scales=(k_scale,))
    init = (jnp.zeros((heads, dk, dv), _F32), jnp.zeros((heads, 1, dk), _F32), jnp.full((heads, 1, 128), ML_M0, _F32))
    scan = functools.partial(_ml_scan, dk=dk, dv=dv)
    _, st_f = scan(None, k_c, v_c, g_c, init, reverse=False)
    _, st_b = scan(None, k_c, v_c, g_c, init, reverse=True)
    hl_f, _ = scan(q_l, k_l, v_l, g_l, st_f, reverse=False)
    hl_b, _ = scan(q_l, k_l, v_l, g_l, st_b, reverse=True)
    merged = _ml_merge(hl_f, hl_b, o_l, norm_w, dv)
    return _mm(merged, w_out.astype(_BF16), name="ml_out")


_SUBLANES = 8


def _sort_network(n):
    pairs = []
    p = 1
    while p < n:
        k = p
        while k >= 1:
            for j in range(k % p, n - k, 2 * k):
                for i in range(min(k, n - j - k)):
                    if (i + j) // (2 * p) == (i + j + k) // (2 * p):
                        pairs.append((i + j, i + j + k))
            k //= 2
        p *= 2
    return pairs


def _top_desc(x, count):
    n = x.shape[0] // _SUBLANES
    tiles = [x[j * _SUBLANES:(j + 1) * _SUBLANES, :] for j in range(n)]
    for a, b in _sort_network(n):
        tiles[a], tiles[b] = jnp.maximum(tiles[a], tiles[b]), jnp.minimum(tiles[a], tiles[b])
    slot = lax.broadcasted_iota(jnp.int32, tiles[0].shape, 0).astype(_F32)
    vals = []
    for it in range(count):
        m = jnp.max(tiles[0], axis=0, keepdims=True)
        vals.append(m)
        remaining = count - it - 1
        if remaining == 0:
            break
        first = jnp.min(jnp.where(tiles[0] == m, slot, float(_SUBLANES)), axis=0, keepdims=True)
        hit = slot == first
        for j in range(min(remaining, n)):
            below = tiles[j + 1] if j + 1 < n else -jnp.inf
            tiles[j] = jnp.where(hit, below, tiles[j])
    return vals


def _peer_route_kernel(q_ref, k1_ref, k2_ref, e1_ref, e2_ref, th_ref):
    half = k1_ref.shape[1]
    q = q_ref[...]
    s1 = _dot(k1_ref[...], q[:, :half], 3, dims=_NT)
    s2 = _dot(k2_ref[...], q[:, half:], 3, dims=_NT)
    n_top = PEER_TOPK + 1
    v1 = _top_desc(s1, n_top)
    v2 = _top_desc(s2, n_top)
    cands = [v1[a] + v2[b] for a in range(n_top) for b in range(n_top) if (a + 1) * (b + 1) <= n_top]
    pad = 64 - len(cands)
    cand = jnp.concatenate(cands + [jnp.full_like(cands[0], -jnp.inf)] * pad, axis=0)
    top = _top_desc(cand, n_top)
    m = top[0]
    z = top[0] * 0.0
    for j in range(PEER_TOPK):
        z = z + jnp.exp(top[j] - m)
    tau = 0.5 * (top[PEER_TOPK - 1] + top[PEER_TOPK])
    inv_z = 1.0 / z
    e1_ref[0] = jnp.exp(s1 - v1[0])
    e2_ref[0] = jnp.exp(s2 - v2[0]) * inv_z
    th_ref[0] = jnp.exp(tau - m) * inv_z


def _cast_t_kernel(x_ref, o_ref):
    o_ref[...] = x_ref[...].T.astype(o_ref.dtype)


def _cast_transpose(a, dtype):
    r, c = a.shape
    tr, tc = _tile(r, 1024, 128), _tile(c, 1024, 128)
    return pl.pallas_call(
        _cast_t_kernel,
        grid=(r // tr, c // tc),
        in_specs=[pl.BlockSpec((tr, tc), lambda i, j: (i, j))],
        out_specs=pl.BlockSpec((tc, tr), lambda i, j: (j, i)),
        out_shape=jax.ShapeDtypeStruct((c, r), dtype),
        compiler_params=_cp("parallel", "parallel"),
        name="cast_t",
    )(a)


def _rowwise_fp8(a):
    amax = jnp.max(jnp.abs(a), axis=1, keepdims=True)
    scale = jnp.maximum(amax, 1e-30) / _FP8_MAX
    return jnp.clip(a / scale, -_FP8_MAX, _FP8_MAX).astype(_FP8), scale


def _peer_dense_kernel(x_ref, sx_ref, u_ref, su_ref, vt_ref, e1_ref, e2_ref, th_ref, o_ref, act_ref, a_ref, *,
                       heads, nkeys):
    @pl.when(pl.program_id(1) == 0)
    def _():
        o_ref[...] = jnp.zeros_like(o_ref)

    n_sub = u_ref.shape[0] // nkeys
    n_part = 2
    part = u_ref.shape[0] // n_part
    lanes = 128

    def activations(pt):
        rows = slice(pt * part, (pt + 1) * part)
        act_ref[rows, :] = lax.dot_general(u_ref[rows, :], x_ref[...], _NT, preferred_element_type=_F32)

    def gated(i):
        rows = slice(i * nkeys, (i + 1) * nkeys)
        for c in range(x_ref.shape[0] // lanes):
            cols = slice(c * lanes, (c + 1) * lanes)
            gate = None
            for h in range(heads):
                p = e1_ref[h, i:i + 1, cols] * e2_ref[h, :, cols]
                sel = jnp.where(p >= th_ref[h, :, cols], p, 0.0)
                gate = sel if gate is None else gate + sel
            act = act_ref[rows, cols] * su_ref[rows, :] * sx_ref[:, cols]
            gelu = 0.5 * act * (1.0 + lax.erf(act * (2.0 ** -0.5)))
            a_ref[rows, cols] = (gate * gelu).astype(a_ref.dtype)

    activations(0)
    for pt in range(n_part):
        if pt + 1 < n_part:
            activations(pt + 1)
        for i in range(pt * n_sub // n_part, (pt + 1) * n_sub // n_part):
            gated(i)
        rows = slice(pt * part, (pt + 1) * part)
        o_ref[...] += jnp.dot(vt_ref[:, rows], a_ref[rows, :], preferred_element_type=_F32)


def _peer(x, wq, k1, k2, u_q, u_scale, vt_bf):
    t, d = x.shape
    heads, nkeys = PEER_HEADS, PEER_NKEYS
    qdim = wq.shape[1] // heads
    q = _mm(x, wq.astype(_BF16), name="peer_q")
    tt = _tile(t, 1024, 128)
    e_shape = jax.ShapeDtypeStruct((heads, nkeys, t), _F32)
    e_blk = pl.BlockSpec((1, nkeys, tt), lambda i, h: (h, 0, i))
    th_blk = pl.BlockSpec((1, 1, tt), lambda i, h: (h, 0, i))
    e1, e2, th = pl.pallas_call(
        _peer_route_kernel,
        grid=(t // tt, heads),
        in_specs=[pl.BlockSpec((tt, qdim), lambda i, h: (i, h)),
                  pl.BlockSpec((nkeys, qdim // 2), lambda i, h: (0, 0)),
                  pl.BlockSpec((nkeys, qdim // 2), lambda i, h: (0, 0))],
        out_specs=[e_blk, e_blk, th_blk],
        out_shape=[e_shape, e_shape, jax.ShapeDtypeStruct((heads, 1, t), _F32)],
        compiler_params=_cp("parallel", "parallel"),
        name="peer_route",
    )(q, k1, k2)
    tb = _tile(t, 1024, 128)
    n_sub = 8
    eb = n_sub * nkeys
    x_q, x_scale = _rowwise_fp8(x)
    return pl.pallas_call(
        functools.partial(_peer_dense_kernel, heads=heads, nkeys=nkeys),
        grid=(t // tb, nkeys // n_sub),
        in_specs=[pl.BlockSpec((tb, d), lambda i, e: (i, 0)),
                  pl.BlockSpec((1, tb), lambda i, e: (0, i)),
                  pl.BlockSpec((eb, d), lambda i, e: (e, 0)),
                  pl.BlockSpec((eb, 128), lambda i, e: (e, 0)),
                  pl.BlockSpec((d, eb), lambda i, e: (0, e)),
                  pl.BlockSpec((heads, n_sub, tb), lambda i, e: (0, e, i)),
                  pl.BlockSpec((heads, nkeys, tb), lambda i, e: (0, 0, i)),
                  pl.BlockSpec((heads, 1, tb), lambda i, e: (0, 0, i))],
        out_specs=pl.BlockSpec((d, tb), lambda i, e: (0, i)),
        out_shape=jax.ShapeDtypeStruct((d, t), _F32),
        scratch_shapes=[pltpu.VMEM((eb, tb), _F32), pltpu.VMEM((eb, tb), _BF16)],
        compiler_params=_cp("parallel", "arbitrary"),
        name="peer_dense",
    )(x_q, x_scale.reshape(1, t), u_q, u_scale, vt_bf, e1, e2, th)


def _hyena_na_mixer(h_lat, h_ctx, mod_l, mod_c, w_in, conv_w, conv_b, fparams, skip, rpb, w_out):
    split = conv_w.shape[1]
    wb = w_in.astype(_BF16)
    w_hy, w_na = wb[:, :split], wb[:, split:]
    outs = []
    qkvs = []
    for hh, (sc, sh) in ((h_lat, mod_l), (h_ctx, mod_c)):
        p_hy = _mm(hh, w_hy, scale=sc, shift=sh, name="hy_proj")
        qkvs.append(_mm(hh, w_na, scale=sc, shift=sh, out_dtype=_BF16, name="na_proj"))
        outs.append(_hyena(p_hy, conv_w, conv_b, fparams, skip))
    na_lat = _natten(qkvs[0], qkvs[1], rpb)
    na_ctx = _ctx_attn(qkvs[1])
    wo = w_out.astype(_BF16)
    y_lat = _mm(jnp.concatenate([outs[0], na_lat], axis=-1), wo, name="ab_out")
    y_ctx = _mm(jnp.concatenate([outs[1], na_ctx], axis=-1), wo, name="ab_out")
    return y_lat, y_ctx


def kernel(x, c, ctx, c_ctx, ada_w, ada_b, ln_w, ln_b, ab_w_in, hy_conv_w, hy_conv_b, hy_fw1, hy_fb1, hy_fw2, hy_fb2, hy_fw3, hy_fb3, hy_fw4, hy_skip, na_rpb, ab_w_out, ml_w_in, ml_b_gates, ml_conv_w, ml_conv_b, ml_norm_w, ml_w_out, peer_wq, peer_k1, peer_k2, peer_u, peer_v):
    bsz, seq_len, d = x.shape
    depth = ada_w.shape[0]
    assert bsz == 1 and depth == 2, "one even (Hyena/attention) layer followed by one last odd (mLSTM) layer"
    alpha = (2 * depth) ** 0.25
    h_lat, h_ctx = x[0], ctx[0]
    cvec = jnp.zeros((8, d), _F32).at[0].set(c[0]).at[1].set(c_ctx)
    row = lambda a: a.reshape(1, d)
    for layer in range(depth):
        last = layer == depth - 1
        mod = _mm(cvec, ada_w[layer], pre="silu", bias=ada_b[layer].reshape(1, 6 * d), passes=3, name="ada_mod")
        sh_l, sc_l, g_l, shf_l, scf_l, gf_l = [mod[0:1, i * d:(i + 1) * d] for i in range(6)]
        sh_c, sc_c, g_c, shf_c, scf_c, gf_c = [mod[1:2, i * d:(i + 1) * d] for i in range(6)]
        if layer % 2 == 0:
            fparams = (hy_fw1[0], hy_fb1[0], hy_fw2[0], hy_fb2[0], hy_fw3[0], hy_fb3[0], hy_fw4[0])
            y_lat, y_ctx = _hyena_na_mixer(h_lat, h_ctx, (sc_l, sh_l), (sc_c, sh_c), ab_w_in[0], hy_conv_w[0],
                                           hy_conv_b[0], fparams, hy_skip[0], na_rpb[0], ab_w_out[0])
        else:
            y_lat = _mlstm_mixer(h_lat, h_ctx, (sc_l, sh_l), (sc_c, sh_c), ml_w_in[0], ml_b_gates[0], ml_conv_w[0],
                                 ml_conv_b[0], ml_norm_w[0], ml_w_out[0])
            y_ctx = None
        lw0, lb0, lw1, lb1 = row(ln_w[layer, 0]), row(ln_b[layer, 0]), row(ln_w[layer, 1]), row(ln_b[layer, 1])
        u_q, u_scale = _rowwise_fp8(peer_u[layer])
        u_scale = jnp.broadcast_to(u_scale, (u_scale.shape[0], 128))
        vt_bf = _cast_transpose(peer_v[layer], _BF16)
        peer = lambda v: _peer(v, peer_wq[layer], peer_k1[layer], peer_k2[layer], u_q, u_scale, vt_bf)
        h_lat, v_lat = _ln_res(h_lat, y_lat, g_l, lw0, lb0, scf_l, shf_l, alpha=alpha, with_mod=True)
        h_lat, _ = _ln_res(h_lat, peer(v_lat), gf_l, lw1, lb1, scf_l, shf_l, alpha=alpha, with_mod=False,
                           y_transposed=True)
        if not last:
            h_ctx, v_ctx = _ln_res(h_ctx, y_ctx, g_c, lw0, lb0, scf_c, shf_c, alpha=alpha, with_mod=True)
            h_ctx, _ = _ln_res(h_ctx, peer(v_ctx), gf_c, lw1, lb1, scf_c, shf_c, alpha=alpha, with_mod=False,
                               y_transposed=True)
    return h_lat[None]
```
